```python
import math
import jax, jax.numpy as jnp
from jax import lax
import numpy as np

D_MODEL = 1024
BATCH = 16
SEQ = 2048
DEPTH = 2

CTX_LEN = 256
GRID_W = 64
NORM_EPS = 1e-6

GDN_HEADS = 8
GDN_HEAD_DIM = 128
GDN_WIDTH = GDN_HEADS * GDN_HEAD_DIM
GDN_CHUNK = 64
QKV_CONV = 5
CMLP_GROUPS = 4
CMLP_CHUNK = 128
CMLP_WIDTH = 512
CMLP_GROUP_DIM = CMLP_WIDTH // CMLP_GROUPS
POOL_WINDOWS = (2, 4, 8, 16)
POOL_GROUPS = 4
POOL_WIDTH = 512
POOL_GROUP_DIM = POOL_WIDTH // POOL_GROUPS
N_BRANCH = 3
IN_SIZES = (GDN_WIDTH, GDN_WIDTH, GDN_WIDTH, GDN_WIDTH, 2 * GDN_HEADS, 2 * GDN_HEADS,
            CMLP_WIDTH, CMLP_WIDTH, POOL_WIDTH, N_BRANCH * D_MODEL)
IN_COLS = sum(IN_SIZES)
N_EXPERTS = 16
EXPERT_FF = 1024
EC_CAPACITY = 2

kernel_name = 'hybrid_gdn_cmlp_pool_ecmoe_dit'


def rms_norm(x, w):
    xf = x.astype(jnp.float32)
    y = xf * lax.rsqrt(jnp.mean(xf * xf, axis=-1, keepdims=True) + NORM_EPS)
    return (y * w.astype(jnp.float32)).astype(x.dtype)


def layer_norm(x, w, b):
    xf = x.astype(jnp.float32)
    mu = jnp.mean(xf, axis=-1, keepdims=True)
    var = jnp.mean(jnp.square(xf - mu), axis=-1, keepdims=True)
    y = (xf - mu) * lax.rsqrt(var + NORM_EPS) * w.astype(jnp.float32) + b.astype(jnp.float32)
    return y.astype(x.dtype)


def l2_normalize(t):
    return t * lax.rsqrt(jnp.sum(t * t, axis=-1, keepdims=True) + NORM_EPS)


def depthwise_conv_centred(x, w):
    pad = w.shape[0] // 2
    return lax.conv_general_dilated(x, w[:, None, :], window_strides=(1,), padding=[(pad, pad)],
                                    dimension_numbers=('NWC', 'WIO', 'NWC'),
                                    feature_group_count=x.shape[-1])


def gated_delta_chunked(q, k, v, g, beta, s0):
    B, L, H, dk = q.shape
    dv = v.shape[-1]
    C = GDN_CHUNK
    N = L // C

    def to_chunks(t):
        t = t.reshape((B, N, C, H) + t.shape[3:])
        return jnp.moveaxis(t, (1, 3), (0, 2))

    qc, kc, vc, gr, bc = (to_chunks(t) for t in (q, k, v, g, beta))
    gc = jnp.cumsum(gr, axis=-1)
    incl = np.tril(np.ones((C, C), dtype=bool))
    strict = np.tril(np.ones((C, C), dtype=bool), -1)
    decay = jnp.exp(jnp.where(incl, gc[..., :, None] - gc[..., None, :], -jnp.inf))
    kb = kc * bc[..., None]
    m = jnp.where(strict, jnp.einsum('nbhik,nbhjk->nbhij', kb, kc) * decay, 0.0)
    a = m + jnp.eye(C, dtype=m.dtype)
    rhs = jnp.concatenate([vc * bc[..., None], kb * jnp.exp(gc)[..., None]], axis=-1)
    sol = lax.linalg.triangular_solve(a, rhs, left_side=True, lower=True, unit_diagonal=True)
    uc, wc = sol[..., :dv], sol[..., dv:]
    attn = jnp.einsum('nbhik,nbhjk->nbhij', qc, kc) * decay

    def step(s, xs):
        q_, k_, u_, w_, g_, at_ = xs
        v_new = u_ - jnp.einsum('bhck,bhkv->bhcv', w_, s)
        o_ = (jnp.einsum('bhck,bhkv->bhcv', q_ * jnp.exp(g_)[..., None], s)
              + jnp.einsum('bhij,bhjv->bhiv', at_, v_new))
        g_last = g_[..., -1:]
        s = (s * jnp.exp(g_last)[..., None]
             + jnp.einsum('bhck,bhcv->bhkv', k_ * jnp.exp(g_last - g_)[..., None], v_new))
        return s, o_

    s_fin, o = lax.scan(step, s0, (qc, kc, uc, wc, gc, attn))
    o = jnp.moveaxis(o, (0, 2), (1, 3)).reshape(B, L, H, dv)
    return o, s_fin


def gdn_branch(q, k, v, z, b_raw, a_raw, init_states, lp):
    B, L, _ = q.shape
    f32 = jnp.float32
    qkv = jax.nn.silu(depthwise_conv_centred(jnp.concatenate([q, k, v], axis=-1), lp['qkv_conv_w']))
    q, k, v = jnp.split(qkv.astype(f32), 3, axis=-1)
    heads = lambda t: t.reshape(B, L, GDN_HEADS, GDN_HEAD_DIM)
    q = l2_normalize(heads(q)) * (GDN_HEAD_DIM ** -0.5)
    k = l2_normalize(heads(k))
    v = heads(v)
    beta = jax.nn.sigmoid(b_raw.astype(f32).reshape(B, L, 2, GDN_HEADS))
    g = (-jnp.exp(lp['gdn_a_log'].astype(f32))
         * jax.nn.softplus(a_raw.astype(f32).reshape(B, L, 2, GDN_HEADS) + lp['gdn_dt_bias'].astype(f32)))
    o_f, s_f = gated_delta_chunked(q, k, v, g[:, :, 0], beta[:, :, 0], init_states[0])
    rev = lambda t: jnp.flip(t, axis=1)
    o_b, s_b = gated_delta_chunked(rev(q), rev(k), rev(v), rev(g[:, :, 1]), rev(beta[:, :, 1]), init_states[1])
    o = o_f + rev(o_b)
    o = o * lax.rsqrt(jnp.mean(o * o, axis=-1, keepdims=True) + NORM_EPS) * lp['gdn_norm_w'].astype(f32)
    o = o * jax.nn.silu(heads(z.astype(f32)))
    return o.reshape(B, L, GDN_WIDTH).astype(z.dtype), (s_f, s_b)


def chunk_mlp_branch(u, vg, lp):
    B, L, _ = u.shape
    n_chunks = L // CMLP_CHUNK
    u = jax.nn.gelu(u)
    vg = layer_norm(jax.nn.gelu(vg), lp['cmlp_ln_w'], lp['cmlp_ln_b'])
    vr = vg.reshape(B, n_chunks, CMLP_CHUNK, CMLP_GROUPS, CMLP_GROUP_DIM)
    mixed = jnp.einsum('gpq,bnqgc->bnpgc', lp['cmlp_w_s'], vr) + lp['cmlp_b_s'].T[:, :, None]
    return u * mixed.reshape(B, L, CMLP_WIDTH)


def pool_branch(p, n_rows, lp):
    B, L, _ = p.shape
    seg = L // n_rows
    pf = p.astype(jnp.float32).reshape(B, n_rows, seg, POOL_GROUPS, POOL_GROUP_DIM)
    cs = jnp.pad(jnp.cumsum(pf, axis=2), ((0, 0), (0, 0), (1, 0), (0, 0), (0, 0)))
    t = np.arange(seg)
    means = []
    for gi, w in enumerate(POOL_WINDOWS):
        lo = np.clip(t - w // 2, 0, seg)
        hi = np.clip(t + w // 2, 0, seg)
        csg = cs[:, :, :, gi]
        win_sum = jnp.take(csg, hi, axis=2) - jnp.take(csg, lo, axis=2)
        means.append(win_sum / (hi - lo).astype(np.float32)[:, None])
    pooled = jnp.stack(means, axis=3) - pf
    mixed = jnp.einsum('brsgc,gcd->brsgd', pooled, lp['pool_w'].astype(jnp.float32))
    return (mixed.reshape(B, L, POOL_WIDTH) * lp['pool_scale'].astype(jnp.float32)).astype(p.dtype)


def hybrid_mixer(h, init_states, n_rows, lp, states_only):
    splits = [int(s) for s in np.cumsum(IN_SIZES)[:-1]]
    proj = jnp.einsum('bld,de->ble', h, lp['w_in'])
    q, k, v, z, b_raw, a_raw, u, vg, p, gate_raw = jnp.split(proj, splits, axis=-1)
    y_a, states = gdn_branch(q, k, v, z, b_raw, a_raw, init_states, lp)
    if states_only:
        return None, states
    y_b = chunk_mlp_branch(u, vg, lp)
    y_c = pool_branch(p, n_rows, lp)
    gates = jax.nn.sigmoid(gate_raw.astype(jnp.float32)).astype(h.dtype)
    g_a, g_b, g_c = jnp.split(gates, N_BRANCH, axis=-1)
    merged = (g_a * jnp.einsum('blc,cd->bld', y_a, lp['w_br_a'])
              + g_b * jnp.einsum('blc,cd->bld', y_b, lp['w_br_b'])
              + g_c * jnp.einsum('blc,cd->bld', y_c, lp['w_br_c']))
    return jnp.einsum('bld,de->ble', merged, lp['w_out']), states


def expert_choice_ffn(h, lp):
    B, n, D = h.shape
    cap = EC_CAPACITY * n // N_EXPERTS
    aff = jax.nn.softmax(jnp.einsum('bnd,de->bne', h, lp['w_router']).astype(jnp.float32), axis=-1)
    top_w, top_i = lax.top_k(jnp.swapaxes(aff, 1, 2), cap)
    xe = jax.vmap(lambda hb, ib: hb[ib])(h, top_i)
    hid = (jax.nn.silu(jnp.einsum('becd,edf->becf', xe, lp['w_gate']))
           * jnp.einsum('becd,edf->becf', xe, lp['w_up']))
    ye = jnp.einsum('becf,efd->becd', hid, lp['w_down']) * top_w[..., None].astype(h.dtype)
    return jax.vmap(lambda ib, yb: jnp.zeros((n, D), yb.dtype).at[ib.reshape(-1)].add(yb.reshape(-1, D)))(top_i, ye)


def setup_inputs(seed: int = 0) -> dict:
    key = jax.random.key(seed)
    ks = iter(jax.random.split(key, 40))
    nrm = lambda shape, scale: jax.random.normal(next(ks), shape, jnp.float32) * scale
    L, D = DEPTH, D_MODEL
    x = nrm((BATCH, SEQ, D), 1.0)
    c = nrm((BATCH, D), 1.0)
    ctx = nrm((BATCH, CTX_LEN, D), 1.0)
    c_ctx = nrm((D,), 1.0)
    w_ada = nrm((L, D, 6 * D), 0.5 * D ** -0.5)
    b_ada = nrm((L, 6 * D), 0.02)
    norm1_w = 1.0 + nrm((L, D), 0.05)
    norm2_w = 1.0 + nrm((L, D), 0.05)
    w_in = nrm((L, D, IN_COLS), D ** -0.5)
    qkv_conv_w = nrm((L, QKV_CONV, 3 * GDN_WIDTH), QKV_CONV ** -0.5)
    gdn_a_log = jnp.log(jax.random.uniform(next(ks), (L, 2, GDN_HEADS), jnp.float32, 1.0, 16.0))
    dt = jnp.exp(jax.random.uniform(next(ks), (L, 2, GDN_HEADS), jnp.float32, math.log(1e-3), math.log(0.1)))
    gdn_dt_bias = dt + jnp.log(-jnp.expm1(-dt))
    gdn_norm_w = 1.0 + nrm((L, GDN_HEAD_DIM), 0.05)
    cmlp_ln_w = 1.0 + nrm((L, CMLP_WIDTH), 0.05)
    cmlp_ln_b = nrm((L, CMLP_WIDTH), 0.02)
    cmlp_w_s = nrm((L, CMLP_GROUPS, CMLP_CHUNK, CMLP_CHUNK), CMLP_CHUNK ** -0.5)
    cmlp_b_s = 1.0 + nrm((L, CMLP_GROUPS, CMLP_CHUNK), 0.05)
    pool_w = nrm((L, POOL_GROUPS, POOL_GROUP_DIM, POOL_GROUP_DIM), POOL_GROUP_DIM ** -0.5)
    pool_scale = 1.0 + nrm((L, POOL_WIDTH), 0.05)
    w_br_a = nrm((L, GDN_WIDTH, D), GDN_WIDTH ** -0.5)
    w_br_b = nrm((L, CMLP_WIDTH, D), CMLP_WIDTH ** -0.5)
    w_br_c = nrm((L, POOL_WIDTH, D), POOL_WIDTH ** -0.5)
    w_out = nrm((L, D, D), D ** -0.5)
    w_router = nrm((L, D, N_EXPERTS), D ** -0.5)
    w_gate = nrm((L, N_EXPERTS, D, EXPERT_FF), D ** -0.5)
    w_up = nrm((L, N_EXPERTS, D, EXPERT_FF), D ** -0.5)
    w_down = nrm((L, N_EXPERTS, EXPERT_FF, D), EXPERT_FF ** -0.5)
    final_norm_w = 1.0 + nrm((D,), 0.05)
    return {'x': x, 'c': c, 'ctx': ctx, 'c_ctx': c_ctx, 'w_ada': w_ada, 'b_ada': b_ada,
            'norm1_w': norm1_w, 'norm2_w': norm2_w, 'w_in': w_in, 'qkv_conv_w': qkv_conv_w,
            'gdn_a_log': gdn_a_log, 'gdn_dt_bias': gdn_dt_bias, 'gdn_norm_w': gdn_norm_w,
            'cmlp_ln_w': cmlp_ln_w, 'cmlp_ln_b': cmlp_ln_b, 'cmlp_w_s': cmlp_w_s, 'cmlp_b_s': cmlp_b_s,
            'pool_w': pool_w, 'pool_scale': pool_scale, 'w_br_a': w_br_a, 'w_br_b': w_br_b,
            'w_br_c': w_br_c, 'w_out': w_out, 'w_router': w_router, 'w_gate': w_gate, 'w_up': w_up,
            'w_down': w_down, 'final_norm_w': final_norm_w}


def reference(x, c, ctx, c_ctx, w_ada, b_ada, norm1_w, norm2_w, w_in, qkv_conv_w, gdn_a_log,
              gdn_dt_bias, gdn_norm_w, cmlp_ln_w, cmlp_ln_b, cmlp_w_s, cmlp_b_s, pool_w, pool_scale,
              w_br_a, w_br_b, w_br_c, w_out, w_router, w_gate, w_up, w_down, final_norm_w):
    B = x.shape[0]
    ROWS = x.shape[1] // GRID_W
    zero_states = (jnp.zeros((B, GDN_HEADS, GDN_HEAD_DIM, GDN_HEAD_DIM), jnp.float32),) * 2
    for l in range(DEPTH):
        last = l == DEPTH - 1
        lp = {'w_in': w_in[l], 'qkv_conv_w': qkv_conv_w[l], 'gdn_a_log': gdn_a_log[l],
              'gdn_dt_bias': gdn_dt_bias[l], 'gdn_norm_w': gdn_norm_w[l], 'cmlp_ln_w': cmlp_ln_w[l],
              'cmlp_ln_b': cmlp_ln_b[l], 'cmlp_w_s': cmlp_w_s[l], 'cmlp_b_s': cmlp_b_s[l],
              'pool_w': pool_w[l], 'pool_scale': pool_scale[l], 'w_br_a': w_br_a[l],
              'w_br_b': w_br_b[l], 'w_br_c': w_br_c[l], 'w_out': w_out[l], 'w_router': w_router[l],
              'w_gate': w_gate[l], 'w_up': w_up[l], 'w_down': w_down[l]}
        mod_x = (jax.nn.silu(c) @ w_ada[l] + b_ada[l])[:, None, :]
        mod_c = (jax.nn.silu(c_ctx) @ w_ada[l] + b_ada[l]).reshape(1, 1, -1)
        sh1x, sc1x, g1x, sh2x, sc2x, g2x = jnp.split(mod_x, 6, axis=-1)
        sh1c, sc1c, g1c, sh2c, sc2c, g2c = jnp.split(mod_c, 6, axis=-1)

        hc = rms_norm(ctx, norm1_w[l]) * (1.0 + sc1c) + sh1c
        ctx_mix, ctx_states = hybrid_mixer(hc, zero_states, 1, lp, states_only=last)
        hx = rms_norm(x, norm1_w[l]) * (1.0 + sc1x) + sh1x
        x_mix, _ = hybrid_mixer(hx, ctx_states, ROWS, lp, states_only=False)
        x = x + g1x * x_mix

        hx2 = rms_norm(x, norm2_w[l]) * (1.0 + sc2x) + sh2x
        x = x + g2x * expert_choice_ffn(hx2, lp)
        if not last:
            ctx = ctx + g1c * ctx_mix
            hc2 = rms_norm(ctx, norm2_w[l]) * (1.0 + sc2c) + sh2c
            ctx = ctx + g2c * expert_choice_ffn(hc2, lp)
    return rms_norm(x, final_norm_w)
```

```python
import functools

import numpy as np
import jax
import jax.numpy as jnp
from jax import lax
from jax.experimental import pallas as pl
from jax.experimental.pallas import tpu as pltpu

F32 = jnp.float32
BF16 = jnp.bfloat16
I32 = jnp.int32

NORM_EPS = 1e-6
GRID_W = 64
HEADS = 8
HEAD_DIM = 128
GDN_WIDTH = HEADS * HEAD_DIM
QKV_CONV = 5
CMLP_GROUPS = 4
CMLP_CHUNK = 128
CMLP_WIDTH = 512
POOL_WINDOWS = (2, 4, 8, 16)
POOL_WIDTH = 512
N_EXPERTS = 16
EC_CAPACITY = 2

LANES = 128
GDN_CHUNK = 128
GDN_HEADS_PER_STEP = 4
INV_PASSES = 3
VMEM_LIMIT = 56 * 1024 * 1024

COL_Q, COL_K, COL_V, COL_Z = 0, 1024, 2048, 3072
COL_GATES = 4096
COL_U, COL_VG, COL_P = 7168, 7680, 8192
COL_BA = 8704
PROJ_COLS = 8960
PROJ_TN = 1280


def _params(sem, vmem=VMEM_LIMIT):
    return pltpu.CompilerParams(dimension_semantics=sem, vmem_limit_bytes=vmem)


def _dot(a, b):
    return jnp.dot(a.astype(BF16), b.astype(BF16), preferred_element_type=F32)


def _split(a):
    hi = a.astype(BF16)
    lo = (a - hi.astype(F32)).astype(BF16)
    return hi, lo


def _dot_hp(a, b, passes):
    if passes == 1:
        return _dot(a, b)
    a_hi, a_lo = _split(a)
    b_hi, b_lo = _split(b)
    mm = lambda x, y: jnp.dot(x, y, preferred_element_type=F32)
    return mm(a_hi, b_hi) + (mm(a_hi, b_lo) + mm(a_lo, b_hi))


def _dot_exact_rhs(a, b_bf16):
    a1 = a.astype(BF16)
    r1 = a - a1.astype(F32)
    a2 = r1.astype(BF16)
    a3 = (r1 - a2.astype(F32)).astype(BF16)
    mm = lambda x: jnp.dot(x, b_bf16, preferred_element_type=F32)
    return mm(a1) + (mm(a2) + mm(a3))


def _sigmoid(x):
    return 1.0 / (1.0 + jnp.exp(-x))


def _silu(x):
    return x * _sigmoid(x)


def _gelu_tanh(x):
    return 0.5 * x * (1.0 + jnp.tanh(0.7978845608028654 * (x + 0.044715 * (x * x * x))))


def _softplus(x):
    return jnp.maximum(x, 0.0) + jnp.log(1.0 + jnp.exp(-jnp.abs(x)))


def _rms(x, w):
    return x * lax.rsqrt(jnp.mean(x * x, axis=-1, keepdims=True) + NORM_EPS) * w


def _mod_kernel(c_ref, w_ref, b_ref, o_ref):
    s = _silu(c_ref[...])
    o_ref[0] = _dot_hp(s, w_ref[0], 3) + b_ref[0]


def _modulation(cc, w_ada, b_ada):
    depth, d, n6 = w_ada.shape
    rows = cc.shape[0]
    tn = 1536
    return pl.pallas_call(
        _mod_kernel,
        grid=(depth, n6 // tn),
        in_specs=[pl.BlockSpec((rows, d), lambda l, j: (0, 0)),
                  pl.BlockSpec((1, d, tn), lambda l, j: (l, 0, j)),
                  pl.BlockSpec((1, 1, tn), lambda l, j: (l, 0, j))],
        out_specs=pl.BlockSpec((1, rows, tn), lambda l, j: (l, 0, j)),
        out_shape=jax.ShapeDtypeStruct((depth, rows, n6), F32),
        name="adaln_mod",
        compiler_params=_params(("arbitrary", "arbitrary")),
    )(cc, w_ada, b_ada.reshape(depth, 1, n6))


def _inproj_kernel(x_ref, sh_ref, sc_ref, nw_ref, w_ref, o_ref, h_s):
    @pl.when(pl.program_id(2) == 0)
    def _():
        h = _rms(x_ref[0], nw_ref[...]) * (1.0 + sc_ref[0]) + sh_ref[0]
        h_s[...] = h.astype(BF16)

    o_ref[0] = jnp.dot(h_s[...], w_ref[...], preferred_element_type=F32)


def _in_projection(x, mod, norm_w, wp):
    b, l, d = x.shape
    tm = min(l, 1024)
    return pl.pallas_call(
        _inproj_kernel,
        grid=(b, l // tm, PROJ_COLS // PROJ_TN),
        in_specs=[pl.BlockSpec((1, tm, d), lambda i, t, j: (i, t, 0)),
                  pl.BlockSpec((1, 1, d), lambda i, t, j: (i, 0, 0)),
                  pl.BlockSpec((1, 1, d), lambda i, t, j: (i, 0, 1)),
                  pl.BlockSpec((1, d), lambda i, t, j: (0, 0)),
                  pl.BlockSpec((d, PROJ_TN), lambda i, t, j: (0, j))],
        out_specs=pl.BlockSpec((1, tm, PROJ_TN), lambda i, t, j: (i, t, j)),
        out_shape=jax.ShapeDtypeStruct((b, l, PROJ_COLS), F32),
        scratch_shapes=[pltpu.VMEM((tm, d), BF16)],
        name="in_proj",
        compiler_params=_params(("arbitrary", "arbitrary", "arbitrary")),
    )(x, mod, mod, norm_w.reshape(1, d), wp)


CONV_PAD = 8


def _conv_kernel(q_ref, k_ref, v_ref, wq_ref, wk_ref, wv_ref, qo_ref, ko_ref, vo_ref, pad_s):
    l = q_ref.shape[1]
    width = q_ref.shape[2]
    zeros = jnp.zeros((CONV_PAD, width), F32)
    pad_s[pl.ds(0, CONV_PAD), :] = zeros
    pad_s[pl.ds(CONV_PAD + l, CONV_PAD), :] = zeros
    half = QKV_CONV // 2

    def conv_silu(x_ref, w_ref):
        pad_s[pl.ds(CONV_PAD, l), :] = x_ref[0]
        acc = None
        for j in range(QKV_CONV):
            term = pad_s[pl.ds(CONV_PAD - half + j, l), :] * w_ref[pl.ds(j, 1), :]
            acc = term if acc is None else acc + term
        return _silu(acc)

    def l2n(y):
        return y * lax.rsqrt(jnp.sum(y * y, axis=-1, keepdims=True) + NORM_EPS)

    yq = conv_silu(q_ref, wq_ref)
    yk = conv_silu(k_ref, wk_ref)
    yv = conv_silu(v_ref, wv_ref)
    for h in range(width // HEAD_DIM):
        sl = slice(h * HEAD_DIM, (h + 1) * HEAD_DIM)
        qo_ref[0, h] = l2n(yq[:, sl]) * (HEAD_DIM ** -0.5)
        ko_ref[0, h] = l2n(yk[:, sl])
        vo_ref[0, h] = yv[:, sl]


def _qkv_conv(proj, conv_w):
    b, l, _ = proj.shape
    width = 2 * HEAD_DIM
    nq = GDN_WIDTH // width
    hm = jax.ShapeDtypeStruct((b, HEADS, l, HEAD_DIM), F32)
    ospec = pl.BlockSpec((1, width // HEAD_DIM, l, HEAD_DIM), lambda i, j: (i, j, 0, 0))
    return pl.pallas_call(
        _conv_kernel,
        grid=(b, nq),
        in_specs=[pl.BlockSpec((1, l, width), lambda i, j: (i, 0, j)),
                  pl.BlockSpec((1, l, width), lambda i, j: (i, 0, nq + j)),
                  pl.BlockSpec((1, l, width), lambda i, j: (i, 0, 2 * nq + j)),
                  pl.BlockSpec((QKV_CONV, width), lambda i, j: (0, j)),
                  pl.BlockSpec((QKV_CONV, width), lambda i, j: (0, nq + j)),
                  pl.BlockSpec((QKV_CONV, width), lambda i, j: (0, 2 * nq + j))],
        out_specs=[ospec, ospec, ospec],
        out_shape=[hm, hm, hm],
        scratch_shapes=[pltpu.VMEM((l + 2 * CONV_PAD, width), F32)],
        name="qkv_conv",
        compiler_params=_params(("arbitrary", "arbitrary")),
    )(proj, proj, proj, conv_w, conv_w, conv_w)


def _gdn_kernel(q_ref, k_ref, v_ref, ba_ref, alog_ref, dtb_ref, s0_ref, o_ref, sfin_ref,
                state_s, beta_s, gc_s, *, heads_per_step, inv_passes):
    c = GDN_CHUNK
    d = pl.program_id(0)
    grp = pl.program_id(2)
    n = pl.program_id(3)

    @pl.when(n == 0)
    def _():
        state_s[...] = s0_ref[0, 0]

    ii = lax.broadcasted_iota(I32, (c, c), 0)
    jj = lax.broadcasted_iota(I32, (c, c), 1)
    sgn = 1 - 2 * d
    dlt = (ii - jj) * sgn
    strict = dlt > 0
    incl = dlt >= 0
    last_idx = (c - 1) * (1 - d)
    xor = ii ^ jj

    beta = _sigmoid(ba_ref[0, 0, 0, 0])
    gdec = -jnp.exp(alog_ref[0]) * _softplus(ba_ref[0, 0, 1, 0] + dtb_ref[0])
    cum01 = jnp.where(dlt <= 0, 1.0, 0.0).astype(BF16)
    beta_s[...] = beta
    gc_s[...] = _dot_exact_rhs(gdec, cum01)

    eye = jnp.where(ii == jj, 1.0, 0.0)
    levels = c.bit_length() - 1

    for i in range(heads_per_step):
        hrow = grp * heads_per_step + i
        b_rows = jnp.broadcast_to(beta_s[pl.ds(hrow, 1), :], (c, c))
        g_rows = jnp.broadcast_to(gc_s[pl.ds(hrow, 1), :], (c, c))
        bcol = b_rows.T
        gcol = g_rows.T
        q = q_ref[0, i]
        k = k_ref[0, i]
        v = v_ref[0, i]

        decay = jnp.exp(jnp.where(incl, gcol - g_rows, -jnp.inf))
        kb = k * bcol
        egc = jnp.exp(gcol)
        gram = lax.dot_general(jnp.concatenate([kb, q], axis=0).astype(BF16), k.astype(BF16),
                               (((1,), (1,)), ((), ())), preferred_element_type=F32)
        m = jnp.where(strict, gram[:c] * decay, 0.0)
        attn = gram[c:] * decay

        t_inv = eye - jnp.where(xor == 1, m, 0.0)
        for lvl in range(1, levels):
            m_off = jnp.where((xor >> lvl) == 1, m, 0.0)
            z = _dot_hp(m_off, t_inv, inv_passes)
            t_inv = t_inv - _dot_hp(t_inv, z, inv_passes)

        rhs = jnp.concatenate([v * bcol, kb * egc], axis=1)
        uw = _dot(t_inv, rhs)
        u = uw[:, :HEAD_DIM]
        w = uw[:, HEAD_DIM:]

        s_prev = state_s[i]
        ws = _dot(jnp.concatenate([w, q * egc], axis=0), s_prev)
        v_new = u - ws[:c]
        o_ref[0, 0, i] = ws[c:] + _dot(attn, v_new)

        g_last = jnp.sum(jnp.where(jj == last_idx, g_rows, 0.0), axis=1, keepdims=True)
        g_last = jnp.broadcast_to(g_last, (c, c))
        kdec = k * jnp.exp(g_last - gcol)
        state_s[i] = s_prev * jnp.exp(g_last) + _dot(kdec.T, v_new)

    @pl.when(n == pl.num_programs(3) - 1)
    def _():
        sfin_ref[0, 0] = state_s[...]


def _gdn(q, k, v, ba_t, alog_b, dtb_b, s0):
    b, h, l, dh = q.shape
    c = GDN_CHUNK
    nch = l // c
    hg = GDN_HEADS_PER_STEP
    chunk = lambda dd, nn: nn + dd * (nch - 1 - 2 * nn)
    qkv_spec = pl.BlockSpec((1, hg, c, dh), lambda dd, i, g, nn: (i, g, chunk(dd, nn), 0))
    st_spec = pl.BlockSpec((1, 1, hg, dh, dh), lambda dd, i, g, nn: (dd, i, g, 0, 0))
    par_spec = pl.BlockSpec((1, h, c), lambda dd, i, g, nn: (dd, 0, 0))
    return pl.pallas_call(
        functools.partial(_gdn_kernel, heads_per_step=hg, inv_passes=INV_PASSES),
        grid=(2, b, h // hg, nch),
        in_specs=[qkv_spec, qkv_spec, qkv_spec,
                  pl.BlockSpec((1, 1, 2, 1, h, c), lambda dd, i, g, nn: (i, chunk(dd, nn), 0, dd, 0, 0)),
                  par_spec, par_spec, st_spec],
        out_specs=[pl.BlockSpec((1, 1, hg, c, dh), lambda dd, i, g, nn: (dd, i, g, chunk(dd, nn), 0)),
                   st_spec],
        out_shape=[jax.ShapeDtypeStruct((2, b, h, l, dh), F32),
                   jax.ShapeDtypeStruct((2, b, h, dh, dh), F32)],
        scratch_shapes=[pltpu.VMEM((hg, dh, dh), F32),
                        pltpu.VMEM((h, c), F32),
                        pltpu.VMEM((h, c), F32)],
        name="gdn",
        compiler_params=_params(("arbitrary", "arbitrary", "arbitrary", "arbitrary")),
    )(q, k, v, ba_t, alog_b, dtb_b, s0)


def _mix_kernel(o_ref, z_ref, ga_ref, gb_ref, gc_ref, u_ref, vg_ref, p_ref, x_ref,
                g1_ref, sh2_ref, sc2_ref, gnw_ref, lnw_ref, lnb_ref, ws_ref, bs_ref,
                pm_ref, icnt_ref, pw_ref, psc_ref, wa_ref, wb_ref, wc_ref, wo_ref,
                n2w_ref, wr_ref, x1_ref, h2_ref, aff_ref):
    tm = x_ref.shape[1]

    z = z_ref[0]
    gnw = gnw_ref[...]
    ya = []
    for h in range(HEADS):
        oh = o_ref[0, 0, h] + o_ref[1, 0, h]
        zh = z[:, h * HEAD_DIM:(h + 1) * HEAD_DIM]
        ya.append(_rms(oh, gnw) * _silu(zh))
    pa = _dot(jnp.concatenate(ya, axis=1), wa_ref[...])

    u = _gelu_tanh(u_ref[0])
    vg = _gelu_tanh(vg_ref[0])
    mu = jnp.mean(vg, axis=-1, keepdims=True)
    var = jnp.mean(jnp.square(vg - mu), axis=-1, keepdims=True)
    vg = (vg - mu) * lax.rsqrt(var + NORM_EPS) * lnw_ref[...] + lnb_ref[...]
    gd = CMLP_WIDTH // CMLP_GROUPS
    rows = []
    for ch in range(tm // CMLP_CHUNK):
        r = slice(ch * CMLP_CHUNK, (ch + 1) * CMLP_CHUNK)
        cols = [_dot(ws_ref[g], vg[r, g * gd:(g + 1) * gd]) + bs_ref[g] for g in range(CMLP_GROUPS)]
        rows.append(jnp.concatenate(cols, axis=1))
    pb = _dot(u * jnp.concatenate(rows, axis=0), wb_ref[...])

    p = p_ref[0]
    psc = psc_ref[...]
    pg = POOL_WIDTH // len(POOL_WINDOWS)
    yc = []
    for g in range(len(POOL_WINDOWS)):
        pcol = p[:, g * pg:(g + 1) * pg]
        pooled = _dot_exact_rhs_lhs(pm_ref[g], pcol) * icnt_ref[g] - pcol
        yc.append(_dot(pooled, pw_ref[g]) * psc[:, g * pg:(g + 1) * pg])
    pc = _dot(jnp.concatenate(yc, axis=1), wc_ref[...])

    merged = _sigmoid(ga_ref[0]) * pa + _sigmoid(gb_ref[0]) * pb + _sigmoid(gc_ref[0]) * pc
    x1 = x_ref[0] + g1_ref[0] * _dot(merged, wo_ref[...])
    x1_ref[0] = x1

    h2 = _rms(x1, n2w_ref[...]) * (1.0 + sc2_ref[0]) + sh2_ref[0]
    h2_ref[0] = h2.astype(BF16)
    logits = _dot_hp(h2, wr_ref[...], 3)
    lane = lax.broadcasted_iota(I32, logits.shape, 1)
    logits = jnp.where(lane < N_EXPERTS, logits, -jnp.inf)
    e = jnp.exp(logits - jnp.max(logits, axis=-1, keepdims=True))
    aff_ref[0] = e / jnp.sum(e, axis=-1, keepdims=True)


def _dot_exact_rhs_lhs(a_bf16, b):
    b1 = b.astype(BF16)
    r1 = b - b1.astype(F32)
    b2 = r1.astype(BF16)
    b3 = (r1 - b2.astype(F32)).astype(BF16)
    mm = lambda y: jnp.dot(a_bf16, y, preferred_element_type=F32)
    return mm(b1) + (mm(b2) + mm(b3))


def _pool_constants(tm, seg):
    t = np.arange(tm)
    pos = t % seg
    base = t - pos
    mats, inv = [], []
    for w in POOL_WINDOWS:
        lo = np.clip(pos - w // 2, 0, seg) + base
        hi = np.clip(pos + w // 2, 0, seg) + base
        s = np.arange(tm)[None, :]
        mats.append(((s >= lo[:, None]) & (s < hi[:, None])).astype(np.float32))
        inv.append(np.broadcast_to((1.0 / (hi - lo).astype(np.float32))[:, None], (tm, POOL_WIDTH // len(POOL_WINDOWS))))
    return jnp.asarray(np.stack(mats), BF16), jnp.asarray(np.stack(inv), F32)


def _mix(o, proj, x, mod, lw, seg):
    b, l, d = x.shape
    tm = min(l, 256)
    pm, icnt = _pool_constants(tm, seg)
    tok = lambda width, blk: pl.BlockSpec((1, tm, width), lambda i, t: (i, t, blk))
    modv = lambda blk: pl.BlockSpec((1, 1, d), lambda i, t: (i, 0, blk))
    full = lambda a: pl.BlockSpec(a.shape, lambda i, t: (0,) * a.ndim)
    consts = [lw['gdn_norm_w'], lw['cmlp_ln_w'], lw['cmlp_ln_b'], lw['cmlp_w_s'], lw['cmlp_b_s'],
              pm, icnt, lw['pool_w'], lw['pool_scale'], lw['w_br_a'], lw['w_br_b'], lw['w_br_c'],
              lw['w_out'], lw['norm2_w'], lw['w_router']]
    return pl.pallas_call(
        _mix_kernel,
        grid=(b, l // tm),
        in_specs=[pl.BlockSpec((2, 1, HEADS, tm, HEAD_DIM), lambda i, t: (0, i, 0, t, 0)),
                  tok(GDN_WIDTH, COL_Z // GDN_WIDTH),
                  tok(d, COL_GATES // d), tok(d, COL_GATES // d + 1), tok(d, COL_GATES // d + 2),
                  tok(CMLP_WIDTH, COL_U // CMLP_WIDTH), tok(CMLP_WIDTH, COL_VG // CMLP_WIDTH),
                  tok(POOL_WIDTH, COL_P // POOL_WIDTH),
                  tok(d, 0),
                  modv(2), modv(3), modv(4)] + [full(a) for a in consts],
        out_specs=[pl.BlockSpec((1, tm, d), lambda i, t: (i, t, 0)),
                   pl.BlockSpec((1, tm, d), lambda i, t: (i, t, 0)),
                   pl.BlockSpec((1, tm, LANES), lambda i, t: (i, t, 0))],
        out_shape=[jax.ShapeDtypeStruct((b, l, d), F32),
                   jax.ShapeDtypeStruct((b, l, d), BF16),
                   jax.ShapeDtypeStruct((b, l, LANES), F32)],
        name="mix",
        compiler_params=_params(("arbitrary", "arbitrary")),
    )(o, proj, proj, proj, proj, proj, proj, proj, x, mod, mod, mod, *consts)


F32_INF_BITS = 0x7F800000


def _route_kernel(a_ref, u_ref, slot_ref, *, cap):
    bits = pltpu.bitcast(a_ref[0], I32)
    capf = float(cap)

    def body(_, carry):
        lo, hi = carry
        mid = lo + ((hi - lo + 1) >> 1)
        cnt = jnp.sum(jnp.where(bits >= mid, 1.0, 0.0), axis=1, keepdims=True)
        ge = cnt >= capf
        return jnp.where(ge, mid, lo), jnp.where(ge, hi, mid - 1)

    rows = bits.shape[0]
    lo, _ = lax.fori_loop(0, 31, body, (jnp.zeros((rows, 1), I32),
                                        jnp.full((rows, 1), F32_INF_BITS, I32)))
    gt = bits > lo
    eq = bits == lo
    need = capf - jnp.sum(jnp.where(gt, 1.0, 0.0), axis=1, keepdims=True)
    upper = u_ref[...]
    eq_before = jnp.dot(jnp.where(eq, 1.0, 0.0).astype(BF16), upper, preferred_element_type=F32)
    sel = gt | (eq & (eq_before < need))
    pos = jnp.dot(jnp.where(sel, 1.0, 0.0).astype(BF16), upper, preferred_element_type=F32)
    slot_ref[0] = jnp.where(sel, pos.astype(I32), -1)


def _route(aff_t, cap):
    b, e, n = aff_t.shape
    upper = jnp.asarray(np.triu(np.ones((n, n), np.float32), 1), BF16)
    return pl.pallas_call(
        functools.partial(_route_kernel, cap=cap),
        grid=(b,),
        in_specs=[pl.BlockSpec((1, e, n), lambda i: (i, 0, 0)),
                  pl.BlockSpec((n, n), lambda i: (0, 0))],
        out_specs=pl.BlockSpec((1, e, n), lambda i: (i, 0, 0)),
        out_shape=jax.ShapeDtypeStruct((b, e, n), I32),
        name="route",
        compiler_params=_params(("arbitrary",)),
    )(aff_t, upper)


MOE_SCATTER_ROWS = 512


def _moe_kernel(h_ref, slot_t_ref, slot_c_ref, aff_ref, wg_ref, wu_ref, wd_ref, o_ref, *, cap):
    e = pl.program_id(1)
    n = h_ref.shape[1]

    @pl.when(e == 0)
    def _():
        o_ref[...] = jnp.zeros_like(o_ref)

    srow = slot_t_ref[0, pl.ds(e, 1), :]
    sel = jnp.where(srow == lax.broadcasted_iota(I32, (cap, n), 0), 1.0, 0.0).astype(BF16)
    xe = jnp.dot(sel, h_ref[0], preferred_element_type=F32).astype(BF16)
    gate = jnp.dot(xe, wg_ref[0], preferred_element_type=F32)
    up = jnp.dot(xe, wu_ref[0], preferred_element_type=F32)
    ye = _dot(_silu(gate) * up, wd_ref[0])
    ye_hi, ye_lo = _split(ye)

    tr = min(n, MOE_SCATTER_ROWS)
    for r in range(n // tr):
        rs = pl.ds(r * tr, tr)
        lane = lax.broadcasted_iota(I32, (tr, N_EXPERTS), 1)
        scol = jnp.sum(jnp.where(lane == e, slot_c_ref[0, rs, :].astype(F32), 0.0), axis=1, keepdims=True)
        acol = jnp.sum(jnp.where(lane == e, aff_ref[0, rs, :], 0.0), axis=1, keepdims=True)
        put = jnp.where(scol == lax.broadcasted_iota(I32, (tr, cap), 1).astype(F32), 1.0, 0.0).astype(BF16)
        contrib = (jnp.dot(put, ye_hi, preferred_element_type=F32)
                   + jnp.dot(put, ye_lo, preferred_element_type=F32))
        o_ref[0, rs, :] += acol * contrib


def _moe(h2, slot_t, slot_c, aff, wg, wu, wd, cap):
    b, n, d = h2.shape
    ff = wg.shape[2]
    return pl.pallas_call(
        functools.partial(_moe_kernel, cap=cap),
        grid=(b, N_EXPERTS),
        in_specs=[pl.BlockSpec((1, n, d), lambda i, e: (i, 0, 0)),
                  pl.BlockSpec((1, N_EXPERTS, n), lambda i, e: (i, 0, 0)),
                  pl.BlockSpec((1, n, N_EXPERTS), lambda i, e: (i, 0, 0)),
                  pl.BlockSpec((1, n, N_EXPERTS), lambda i, e: (i, 0, 0)),
                  pl.BlockSpec((1, d, ff), lambda i, e: (e, 0, 0)),
                  pl.BlockSpec((1, d, ff), lambda i, e: (e, 0, 0)),
                  pl.BlockSpec((1, ff, d), lambda i, e: (e, 0, 0))],
        out_specs=pl.BlockSpec((1, n, d), lambda i, e: (i, 0, 0)),
        out_shape=jax.ShapeDtypeStruct((b, n, d), F32),
        name="moe",
        compiler_params=_params(("arbitrary", "arbitrary")),
    )(h2, slot_t, slot_c, aff, wg, wu, wd)


def _resid_kernel(x_ref, y_ref, g_ref, w_ref, o_ref, *, final_norm):
    x = x_ref[0] + g_ref[0] * y_ref[0]
    o_ref[0] = _rms(x, w_ref[...]) if final_norm else x


def _residual(x1, moe, mod, final_w, final_norm):
    b, l, d = x1.shape
    tm = min(l, 512)
    tok = pl.BlockSpec((1, tm, d), lambda i, t: (i, t, 0))
    return pl.pallas_call(
        functools.partial(_resid_kernel, final_norm=final_norm),
        grid=(b, l // tm),
        in_specs=[tok, tok, pl.BlockSpec((1, 1, d), lambda i, t: (i, 0, 5)),
                  pl.BlockSpec((1, d), lambda i, t: (0, 0))],
        out_specs=tok,
        out_shape=jax.ShapeDtypeStruct((b, l, d), F32),
        name="residual",
        compiler_params=_params(("arbitrary", "arbitrary")),
    )(x1, moe, mod, final_w.reshape(1, d))


def _ba_rows(proj):
    b, l, _ = proj.shape
    ba = proj[:, :, COL_BA:COL_BA + 4 * HEADS]
    ba = ba.reshape(b, l // GDN_CHUNK, GDN_CHUNK, 2, 2, HEADS)
    return jnp.transpose(ba, (0, 1, 3, 4, 5, 2))


def _channel_ffn(h2, aff, x1, mod, lw, final_w, final_norm):
    b, n, _ = h2.shape
    cap = EC_CAPACITY * n // N_EXPERTS
    aff16 = aff[:, :, :N_EXPERTS]
    slot_t = _route(jnp.swapaxes(aff16, 1, 2), cap)
    moe = _moe(h2, slot_t, jnp.swapaxes(slot_t, 1, 2), aff16, lw['w_gate'], lw['w_up'], lw['w_down'], cap)
    return _residual(x1, moe, mod, final_w, final_norm)


def kernel(x, c, ctx, c_ctx, w_ada, b_ada, norm1_w, norm2_w, w_in, qkv_conv_w, gdn_a_log, gdn_dt_bias, gdn_norm_w, cmlp_ln_w, cmlp_ln_b, cmlp_w_s, cmlp_b_s, pool_w, pool_scale, w_br_a, w_br_b, w_br_c, w_out, w_router, w_gate, w_up, w_down, final_norm_w):
    b, l, d = x.shape
    depth = w_ada.shape[0]
    assert l % GDN_CHUNK == 0 and ctx.shape[1] % GDN_CHUNK == 0 and l % GRID_W == 0

    cc = jnp.concatenate([c, c_ctx[None, :], jnp.zeros((31 - b, d), F32)], axis=0)
    mods = _modulation(cc, w_ada, b_ada)
    zero_states = jnp.zeros((2, b, HEADS, HEAD_DIM, HEAD_DIM), F32)

    for layer in range(depth):
        last = layer == depth - 1
        mod_x = mods[layer, :b][:, None, :]
        mod_c = jnp.broadcast_to(mods[layer, b][None, None, :], mod_x.shape)

        wi = w_in[layer]
        wp = jnp.concatenate([wi[:, 0:4096], wi[:, 5664:8736], wi[:, 4128:5664], wi[:, 4096:4128],
                              jnp.zeros((d, PROJ_COLS - COL_BA - 4 * HEADS), F32)], axis=1).astype(BF16)
        wr = jnp.concatenate([w_router[layer], jnp.zeros((d, LANES - N_EXPERTS), F32)], axis=1)
        lw = {
            'gdn_norm_w': gdn_norm_w[layer].reshape(1, HEAD_DIM),
            'cmlp_ln_w': cmlp_ln_w[layer].reshape(1, CMLP_WIDTH),
            'cmlp_ln_b': cmlp_ln_b[layer].reshape(1, CMLP_WIDTH),
            'cmlp_w_s': cmlp_w_s[layer].astype(BF16),
            'cmlp_b_s': jnp.broadcast_to(cmlp_b_s[layer][:, :, None],
                                         (CMLP_GROUPS, CMLP_CHUNK, CMLP_WIDTH // CMLP_GROUPS)),
            'pool_w': pool_w[layer].astype(BF16),
            'pool_scale': pool_scale[layer].reshape(1, POOL_WIDTH),
            'w_br_a': w_br_a[layer].astype(BF16), 'w_br_b': w_br_b[layer].astype(BF16),
            'w_br_c': w_br_c[layer].astype(BF16), 'w_out': w_out[layer].astype(BF16),
            'norm2_w': norm2_w[layer].reshape(1, d), 'w_router': wr,
            'w_gate': w_gate[layer].astype(BF16), 'w_up': w_up[layer].astype(BF16),
            'w_down': w_down[layer].astype(BF16),
        }
        alog_b = jnp.broadcast_to(gdn_a_log[layer][:, :, None], (2, HEADS, GDN_CHUNK))
        dtb_b = jnp.broadcast_to(gdn_dt_bias[layer][:, :, None], (2, HEADS, GDN_CHUNK))

        def gdn_path(tokens, mod, states):
            proj = _in_projection(tokens, mod, norm1_w[layer], wp)
            q, k, v = _qkv_conv(proj, qkv_conv_w[layer])
            o, s_fin = _gdn(q, k, v, _ba_rows(proj), alog_b, dtb_b, states)
            return proj, o, s_fin

        proj_c, o_c, ctx_states = gdn_path(ctx, mod_c, zero_states)
        proj_x, o_x, _ = gdn_path(x, mod_x, ctx_states)
        x1, h2, aff = _mix(o_x, proj_x, x, mod_x, lw, GRID_W)
        x = _channel_ffn(h2, aff, x1, mod_x, lw, final_norm_w, last)
        if not last:
            c1, h2c, affc = _mix(o_c, proj_c, ctx, mod_c, lw, ctx.shape[1])
            ctx = _channel_ffn(h2c, affc, c1, mod_c, lw, final_norm_w, False)
    return x
```

```python
import functools

import numpy as np
import jax
import jax.numpy as jnp
from jax import lax
from jax.experimental import pallas as pl
from jax.experimental.pallas import tpu as pltpu

F32 = jnp.float32
BF16 = jnp.bfloat16
I32 = jnp.int32

NORM_EPS = 1e-6
GRID_W = 64
HEADS = 8
HEAD_DIM = 128
GDN_WIDTH = HEADS * HEAD_DIM
QKV_CONV = 5
CMLP_GROUPS = 4
CMLP_CHUNK = 128
CMLP_WIDTH = 512
POOL_WINDOWS = (2, 4, 8, 16)
POOL_WIDTH = 512
N_EXPERTS = 16
EC_CAPACITY = 2

LANES = 128
GDN_CHUNK = 128
GDN_HEADS_PER_STEP = 8
INV_PASSES = 1
VMEM_LIMIT = 56 * 1024 * 1024

COL_Q, COL_K, COL_V, COL_Z = 0, 1024, 2048, 3072
COL_GATES = 4096
COL_U, COL_VG, COL_P = 7168, 7680, 8192
COL_BA = 8704
PROJ_COLS = 8960
PROJ_TN = 1280


def _params(sem, vmem=VMEM_LIMIT):
    return pltpu.CompilerParams(dimension_semantics=sem, vmem_limit_bytes=vmem)


def _dot(a, b):
    return jnp.dot(a.astype(BF16), b.astype(BF16), preferred_element_type=F32)


def _split(a):
    hi = a.astype(BF16)
    lo = (a - hi.astype(F32)).astype(BF16)
    return hi, lo


def _dot_hp(a, b, passes):
    if passes == 1:
        return _dot(a, b)
    a_hi, a_lo = _split(a)
    b_hi, b_lo = _split(b)
    mm = lambda x, y: jnp.dot(x, y, preferred_element_type=F32)
    return mm(a_hi, b_hi) + (mm(a_hi, b_lo) + mm(a_lo, b_hi))


def _dot_exact_rhs(a, b_bf16):
    a1 = a.astype(BF16)
    r1 = a - a1.astype(F32)
    a2 = r1.astype(BF16)
    a3 = (r1 - a2.astype(F32)).astype(BF16)
    mm = lambda x: jnp.dot(x, b_bf16, preferred_element_type=F32)
    return mm(a1) + (mm(a2) + mm(a3))


def _sigmoid(x):
    return 1.0 / (1.0 + jnp.exp(-x))


def _silu(x):
    return x * _sigmoid(x)


def _gelu_tanh(x):
    return 0.5 * x * (1.0 + jnp.tanh(0.7978845608028654 * (x + 0.044715 * (x * x * x))))


def _softplus(x):
    return jnp.maximum(x, 0.0) + jnp.log(1.0 + jnp.exp(-jnp.abs(x)))


def _rms(x, w):
    return x * lax.rsqrt(jnp.mean(x * x, axis=-1, keepdims=True) + NORM_EPS) * w


def _mod_kernel(c_ref, w_ref, b_ref, o_ref):
    s = _silu(c_ref[...])
    o_ref[0] = _dot_hp(s, w_ref[0], 3) + b_ref[0]


def _modulation(cc, w_ada, b_ada):
    depth, d, n6 = w_ada.shape
    rows = cc.shape[0]
    tn = 1536
    return pl.pallas_call(
        _mod_kernel,
        grid=(depth, n6 // tn),
        in_specs=[pl.BlockSpec((rows, d), lambda l, j: (0, 0)),
                  pl.BlockSpec((1, d, tn), lambda l, j: (l, 0, j)),
                  pl.BlockSpec((1, 1, tn), lambda l, j: (l, 0, j))],
        out_specs=pl.BlockSpec((1, rows, tn), lambda l, j: (l, 0, j)),
        out_shape=jax.ShapeDtypeStruct((depth, rows, n6), F32),
        name="adaln_mod",
        compiler_params=_params(("arbitrary", "arbitrary")),
    )(cc, w_ada, b_ada.reshape(depth, 1, n6))


def _inproj_kernel(x_ref, sh_ref, sc_ref, nw_ref, w_ref, o_ref, h_s):
    @pl.when(pl.program_id(2) == 0)
    def _():
        h = _rms(x_ref[0], nw_ref[...]) * (1.0 + sc_ref[0]) + sh_ref[0]
        h_s[...] = h.astype(BF16)

    o_ref[0] = jnp.dot(h_s[...], w_ref[...], preferred_element_type=F32)


def _in_projection(x, mod, norm_w, wp):
    b, l, d = x.shape
    tm = min(l, 1024)
    return pl.pallas_call(
        _inproj_kernel,
        grid=(b, l // tm, PROJ_COLS // PROJ_TN),
        in_specs=[pl.BlockSpec((1, tm, d), lambda i, t, j: (i, t, 0)),
                  pl.BlockSpec((1, 1, d), lambda i, t, j: (i, 0, 0)),
                  pl.BlockSpec((1, 1, d), lambda i, t, j: (i, 0, 1)),
                  pl.BlockSpec((1, d), lambda i, t, j: (0, 0)),
                  pl.BlockSpec((d, PROJ_TN), lambda i, t, j: (0, j))],
        out_specs=pl.BlockSpec((1, tm, PROJ_TN), lambda i, t, j: (i, t, j)),
        out_shape=jax.ShapeDtypeStruct((b, l, PROJ_COLS), F32),
        scratch_shapes=[pltpu.VMEM((tm, d), BF16)],
        name="in_proj",
        compiler_params=_params(("arbitrary", "arbitrary", "arbitrary")),
    )(x, mod, mod, norm_w.reshape(1, d), wp)


CONV_PAD = 8


def _conv_kernel(q_ref, k_ref, v_ref, wq_ref, wk_ref, wv_ref, qo_ref, ko_ref, vo_ref, pad_s):
    l = q_ref.shape[1]
    width = q_ref.shape[2]
    zeros = jnp.zeros((CONV_PAD, width), F32)
    pad_s[pl.ds(0, CONV_PAD), :] = zeros
    pad_s[pl.ds(CONV_PAD + l, CONV_PAD), :] = zeros
    half = QKV_CONV // 2

    def conv_silu(x_ref, w_ref):
        pad_s[pl.ds(CONV_PAD, l), :] = x_ref[0]
        acc = None
        for j in range(QKV_CONV):
            term = pad_s[pl.ds(CONV_PAD - half + j, l), :] * w_ref[pl.ds(j, 1), :]
            acc = term if acc is None else acc + term
        return _silu(acc)

    def l2n(y):
        return y * lax.rsqrt(jnp.sum(y * y, axis=-1, keepdims=True) + NORM_EPS)

    yq = conv_silu(q_ref, wq_ref)
    yk = conv_silu(k_ref, wk_ref)
    yv = conv_silu(v_ref, wv_ref)
    for h in range(width // HEAD_DIM):
        sl = slice(h * HEAD_DIM, (h + 1) * HEAD_DIM)
        qo_ref[0, h] = l2n(yq[:, sl]) * (HEAD_DIM ** -0.5)
        ko_ref[0, h] = l2n(yk[:, sl])
        vo_ref[0, h] = yv[:, sl]


def _qkv_conv(proj, conv_w):
    b, l, _ = proj.shape
    width = 2 * HEAD_DIM
    nq = GDN_WIDTH // width
    hm = jax.ShapeDtypeStruct((b, HEADS, l, HEAD_DIM), F32)
    ospec = pl.BlockSpec((1, width // HEAD_DIM, l, HEAD_DIM), lambda i, j: (i, j, 0, 0))
    return pl.pallas_call(
        _conv_kernel,
        grid=(b, nq),
        in_specs=[pl.BlockSpec((1, l, width), lambda i, j: (i, 0, j)),
                  pl.BlockSpec((1, l, width), lambda i, j: (i, 0, nq + j)),
                  pl.BlockSpec((1, l, width), lambda i, j: (i, 0, 2 * nq + j)),
                  pl.BlockSpec((QKV_CONV, width), lambda i, j: (0, j)),
                  pl.BlockSpec((QKV_CONV, width), lambda i, j: (0, nq + j)),
                  pl.BlockSpec((QKV_CONV, width), lambda i, j: (0, 2 * nq + j))],
        out_specs=[ospec, ospec, ospec],
        out_shape=[hm, hm, hm],
        scratch_shapes=[pltpu.VMEM((l + 2 * CONV_PAD, width), F32)],
        name="qkv_conv",
        compiler_params=_params(("arbitrary", "arbitrary")),
    )(proj, proj, proj, conv_w, conv_w, conv_w)


def _gdn_kernel(q_ref, k_ref, v_ref, ba_ref, alog_ref, dtb_ref, s0_ref, o_ref, sfin_ref,
                state_s, beta_s, gc_s, *, heads_per_step, inv_passes):
    c = GDN_CHUNK
    d = pl.program_id(0)
    grp = pl.program_id(2)
    n = pl.program_id(3)

    @pl.when(n == 0)
    def _():
        state_s[...] = s0_ref[0, 0]

    ii = lax.broadcasted_iota(I32, (c, c), 0)
    jj = lax.broadcasted_iota(I32, (c, c), 1)
    sgn = 1 - 2 * d
    dlt = (ii - jj) * sgn
    strict = dlt > 0
    incl = dlt >= 0
    last_idx = (c - 1) * (1 - d)
    xor = ii ^ jj

    beta = _sigmoid(ba_ref[0, 0, 0, 0])
    gdec = -jnp.exp(alog_ref[0]) * _softplus(ba_ref[0, 0, 1, 0] + dtb_ref[0])
    cum01 = jnp.where(dlt <= 0, 1.0, 0.0).astype(BF16)
    beta_s[...] = beta
    gc_s[...] = _dot_exact_rhs(gdec, cum01)

    eye = jnp.where(ii == jj, 1.0, 0.0)
    levels = c.bit_length() - 1

    hs = range(heads_per_step)
    g_rows, bcol, gcol = [], [], []
    for i in hs:
        hrow = grp * heads_per_step + i
        b_rows = jnp.broadcast_to(beta_s[pl.ds(hrow, 1), :], (c, c))
        g_rows.append(jnp.broadcast_to(gc_s[pl.ds(hrow, 1), :], (c, c)))
        bcol.append(b_rows.T)
        gcol.append(g_rows[i].T)
    kb = [k_ref[0, i] * bcol[i] for i in hs]
    gram = [lax.dot_general(jnp.concatenate([kb[i], q_ref[0, i]], axis=0).astype(BF16),
                            k_ref[0, i].astype(BF16), (((1,), (1,)), ((), ())),
                            preferred_element_type=F32) for i in hs]
    decay = [jnp.exp(jnp.where(incl, gcol[i] - g_rows[i], -jnp.inf)) for i in hs]
    m = [jnp.where(strict, gram[i][:c] * decay[i], 0.0) for i in hs]
    attn = [gram[i][c:] * decay[i] for i in hs]

    t_inv = [eye - jnp.where(xor == 1, m[i], 0.0) for i in hs]
    for lvl in range(1, levels):
        z = [_dot_hp(jnp.where((xor >> lvl) == 1, m[i], 0.0), t_inv[i], inv_passes) for i in hs]
        t_inv = [t_inv[i] - _dot_hp(t_inv[i], z[i], inv_passes) for i in hs]

    egc = [jnp.exp(gcol[i]) for i in hs]
    uw = [_dot(t_inv[i], jnp.concatenate([v_ref[0, i] * bcol[i], kb[i] * egc[i]], axis=1)) for i in hs]
    s_prev = [state_s[i] for i in hs]
    ws = [_dot(jnp.concatenate([uw[i][:, HEAD_DIM:], q_ref[0, i] * egc[i]], axis=0), s_prev[i]) for i in hs]
    v_new = [uw[i][:, :HEAD_DIM] - ws[i][:c] for i in hs]
    av = [_dot(attn[i], v_new[i]) for i in hs]
    for i in hs:
        o_ref[0, 0, i] = ws[i][c:] + av[i]

    g_last = [jnp.broadcast_to(jnp.sum(jnp.where(jj == last_idx, g_rows[i], 0.0), axis=1, keepdims=True),
                               (c, c)) for i in hs]
    kdec_t = [(k_ref[0, i] * jnp.exp(g_last[i] - gcol[i])).T for i in hs]
    upd = [_dot(kdec_t[i], v_new[i]) for i in hs]
    for i in hs:
        state_s[i] = s_prev[i] * jnp.exp(g_last[i]) + upd[i]

    @pl.when(n == pl.num_programs(3) - 1)
    def _():
        sfin_ref[0, 0] = state_s[...]


def _gdn(q, k, v, ba_t, alog_b, dtb_b, s0):
    b, h, l, dh = q.shape
    c = GDN_CHUNK
    nch = l // c
    hg = GDN_HEADS_PER_STEP
    chunk = lambda dd, nn: nn + dd * (nch - 1 - 2 * nn)
    qkv_spec = pl.BlockSpec((1, hg, c, dh), lambda dd, i, g, nn: (i, g, chunk(dd, nn), 0))
    st_spec = pl.BlockSpec((1, 1, hg, dh, dh), lambda dd, i, g, nn: (dd, i, g, 0, 0))
    par_spec = pl.BlockSpec((1, h, c), lambda dd, i, g, nn: (dd, 0, 0))
    return pl.pallas_call(
        functools.partial(_gdn_kernel, heads_per_step=hg, inv_passes=INV_PASSES),
        grid=(2, b, h // hg, nch),
        in_specs=[qkv_spec, qkv_spec, qkv_spec,
                  pl.BlockSpec((1, 1, 2, 1, h, c), lambda dd, i, g, nn: (i, chunk(dd, nn), 0, dd, 0, 0)),
                  par_spec, par_spec, st_spec],
        out_specs=[pl.BlockSpec((1, 1, hg, c, dh), lambda dd, i, g, nn: (dd, i, g, chunk(dd, nn), 0)),
                   st_spec],
        out_shape=[jax.ShapeDtypeStruct((2, b, h, l, dh), F32),
                   jax.ShapeDtypeStruct((2, b, h, dh, dh), F32)],
        scratch_shapes=[pltpu.VMEM((hg, dh, dh), F32),
                        pltpu.VMEM((h, c), F32),
                        pltpu.VMEM((h, c), F32)],
        name="gdn",
        compiler_params=_params(("arbitrary", "arbitrary", "arbitrary", "arbitrary")),
    )(q, k, v, ba_t, alog_b, dtb_b, s0)


def _mix_kernel(o_ref, z_ref, ga_ref, gb_ref, gc_ref, u_ref, vg_ref, p_ref, x_ref,
                g1_ref, sh2_ref, sc2_ref, gnw_ref, lnw_ref, lnb_ref, ws_ref, bs_ref,
                pm_ref, icnt_ref, pw_ref, psc_ref, wa_ref, wb_ref, wc_ref, wo_ref,
                n2w_ref, wr_ref, x1_ref, h2_ref, aff_ref):
    tm = x_ref.shape[1]

    z = z_ref[0]
    gnw = gnw_ref[...]
    ya = []
    for h in range(HEADS):
        oh = o_ref[0, 0, h] + o_ref[1, 0, h]
        zh = z[:, h * HEAD_DIM:(h + 1) * HEAD_DIM]
        ya.append(_rms(oh, gnw) * _silu(zh))
    pa = _dot(jnp.concatenate(ya, axis=1), wa_ref[...])

    u = _gelu_tanh(u_ref[0])
    vg = _gelu_tanh(vg_ref[0])
    mu = jnp.mean(vg, axis=-1, keepdims=True)
    var = jnp.mean(jnp.square(vg - mu), axis=-1, keepdims=True)
    vg = (vg - mu) * lax.rsqrt(var + NORM_EPS) * lnw_ref[...] + lnb_ref[...]
    gd = CMLP_WIDTH // CMLP_GROUPS
    rows = []
    for ch in range(tm // CMLP_CHUNK):
        r = slice(ch * CMLP_CHUNK, (ch + 1) * CMLP_CHUNK)
        cols = [_dot(ws_ref[g], vg[r, g * gd:(g + 1) * gd]) + bs_ref[g] for g in range(CMLP_GROUPS)]
        rows.append(jnp.concatenate(cols, axis=1))
    pb = _dot(u * jnp.concatenate(rows, axis=0), wb_ref[...])

    p = p_ref[0]
    psc = psc_ref[...]
    pg = POOL_WIDTH // len(POOL_WINDOWS)
    yc = []
    for g in range(len(POOL_WINDOWS)):
        pcol = p[:, g * pg:(g + 1) * pg]
        pooled = _dot_exact_rhs_lhs(pm_ref[g], pcol) * icnt_ref[g] - pcol
        yc.append(_dot(pooled, pw_ref[g]) * psc[:, g * pg:(g + 1) * pg])
    pc = _dot(jnp.concatenate(yc, axis=1), wc_ref[...])

    merged = _sigmoid(ga_ref[0]) * pa + _sigmoid(gb_ref[0]) * pb + _sigmoid(gc_ref[0]) * pc
    x1 = x_ref[0] + g1_ref[0] * _dot(merged, wo_ref[...])
    x1_ref[0] = x1

    h2 = _rms(x1, n2w_ref[...]) * (1.0 + sc2_ref[0]) + sh2_ref[0]
    h2_ref[0] = h2.astype(BF16)
    logits = _dot_hp(h2, wr_ref[...], 3)
    lane = lax.broadcasted_iota(I32, logits.shape, 1)
    logits = jnp.where(lane < N_EXPERTS, logits, -jnp.inf)
    e = jnp.exp(logits - jnp.max(logits, axis=-1, keepdims=True))
    aff_ref[0] = e / jnp.sum(e, axis=-1, keepdims=True)


def _dot_exact_rhs_lhs(a_bf16, b):
    b1 = b.astype(BF16)
    r1 = b - b1.astype(F32)
    b2 = r1.astype(BF16)
    b3 = (r1 - b2.astype(F32)).astype(BF16)
    mm = lambda y: jnp.dot(a_bf16, y, preferred_element_type=F32)
    return mm(b1) + (mm(b2) + mm(b3))


def _pool_constants(tm, seg):
    t = np.arange(tm)
    pos = t % seg
    base = t - pos
    mats, inv = [], []
    for w in POOL_WINDOWS:
        lo = np.clip(pos - w // 2, 0, seg) + base
        hi = np.clip(pos + w // 2, 0, seg) + base
        s = np.arange(tm)[None, :]
        mats.append(((s >= lo[:, None]) & (s < hi[:, None])).astype(np.float32))
        inv.append(np.broadcast_to((1.0 / (hi - lo).astype(np.float32))[:, None], (tm, POOL_WIDTH // len(POOL_WINDOWS))))
    return jnp.asarray(np.stack(mats), BF16), jnp.asarray(np.stack(inv), F32)


def _mix(o, proj, x, mod, lw, seg):
    b, l, d = x.shape
    tm = min(l, 256)
    pm, icnt = _pool_constants(tm, seg)
    tok = lambda width, blk: pl.BlockSpec((1, tm, width), lambda i, t: (i, t, blk))
    modv = lambda blk: pl.BlockSpec((1, 1, d), lambda i, t: (i, 0, blk))
    full = lambda a: pl.BlockSpec(a.shape, lambda i, t: (0,) * a.ndim)
    consts = [lw['gdn_norm_w'], lw['cmlp_ln_w'], lw['cmlp_ln_b'], lw['cmlp_w_s'], lw['cmlp_b_s'],
              pm, icnt, lw['pool_w'], lw['pool_scale'], lw['w_br_a'], lw['w_br_b'], lw['w_br_c'],
              lw['w_out'], lw['norm2_w'], lw['w_router']]
    return pl.pallas_call(
        _mix_kernel,
        grid=(b, l // tm),
        in_specs=[pl.BlockSpec((2, 1, HEADS, tm, HEAD_DIM), lambda i, t: (0, i, 0, t, 0)),
                  tok(GDN_WIDTH, COL_Z // GDN_WIDTH),
                  tok(d, COL_GATES // d), tok(d, COL_GATES // d + 1), tok(d, COL_GATES // d + 2),
                  tok(CMLP_WIDTH, COL_U // CMLP_WIDTH), tok(CMLP_WIDTH, COL_VG // CMLP_WIDTH),
                  tok(POOL_WIDTH, COL_P // POOL_WIDTH),
                  tok(d, 0),
                  modv(2), modv(3), modv(4)] + [full(a) for a in consts],
        out_specs=[pl.BlockSpec((1, tm, d), lambda i, t: (i, t, 0)),
                   pl.BlockSpec((1, tm, d), lambda i, t: (i, t, 0)),
                   pl.BlockSpec((1, tm, LANES), lambda i, t: (i, t, 0))],
        out_shape=[jax.ShapeDtypeStruct((b, l, d), F32),
                   jax.ShapeDtypeStruct((b, l, d), BF16),
                   jax.ShapeDtypeStruct((b, l, LANES), F32)],
        name="mix",
        compiler_params=_params(("arbitrary", "arbitrary")),
    )(o, proj, proj, proj, proj, proj, proj, proj, x, mod, mod, mod, *consts)


F32_INF_BITS = 0x7F800000


def _route_kernel(a_ref, u_ref, slot_ref, *, cap):
    bits = pltpu.bitcast(a_ref[0], I32)
    capf = float(cap)

    def body(_, carry):
        lo, hi = carry
        mid = lo + ((hi - lo + 1) >> 1)
        cnt = jnp.sum(jnp.where(bits >= mid, 1.0, 0.0), axis=1, keepdims=True)
        ge = cnt >= capf
        return jnp.where(ge, mid, lo), jnp.where(ge, hi, mid - 1)

    rows = bits.shape[0]
    lo, _ = lax.fori_loop(0, 31, body, (jnp.zeros((rows, 1), I32),
                                        jnp.full((rows, 1), F32_INF_BITS, I32)))
    gt = bits > lo
    eq = bits == lo
    need = capf - jnp.sum(jnp.where(gt, 1.0, 0.0), axis=1, keepdims=True)
    upper = u_ref[...]
    eq_before = jnp.dot(jnp.where(eq, 1.0, 0.0).astype(BF16), upper, preferred_element_type=F32)
    sel = gt | (eq & (eq_before < need))
    pos = jnp.dot(jnp.where(sel, 1.0, 0.0).astype(BF16), upper, preferred_element_type=F32)
    slot_ref[0] = jnp.where(sel, pos.astype(I32), -1)


def _route(aff_t, cap):
    b, e, n = aff_t.shape
    upper = jnp.asarray(np.triu(np.ones((n, n), np.float32), 1), BF16)
    return pl.pallas_call(
        functools.partial(_route_kernel, cap=cap),
        grid=(b,),
        in_specs=[pl.BlockSpec((1, e, n), lambda i: (i, 0, 0)),
                  pl.BlockSpec((n, n), lambda i: (0, 0))],
        out_specs=pl.BlockSpec((1, e, n), lambda i: (i, 0, 0)),
        out_shape=jax.ShapeDtypeStruct((b, e, n), I32),
        name="route",
        compiler_params=_params(("arbitrary",)),
    )(aff_t, upper)


MOE_SCATTER_ROWS = 512


def _moe_kernel(h_ref, slot_t_ref, slot_c_ref, aff_ref, wg_ref, wu_ref, wd_ref, o_ref, *, cap):
    e = pl.program_id(1)
    n = h_ref.shape[1]

    @pl.when(e == 0)
    def _():
        o_ref[...] = jnp.zeros_like(o_ref)

    srow = slot_t_ref[0, pl.ds(e, 1), :]
    sel = jnp.where(srow == lax.broadcasted_iota(I32, (cap, n), 0), 1.0, 0.0).astype(BF16)
    xe = jnp.dot(sel, h_ref[0], preferred_element_type=F32).astype(BF16)
    gate = jnp.dot(xe, wg_ref[0], preferred_element_type=F32)
    up = jnp.dot(xe, wu_ref[0], preferred_element_type=F32)
    ye = _dot(_silu(gate) * up, wd_ref[0])
    ye_hi, ye_lo = _split(ye)

    tr = min(n, MOE_SCATTER_ROWS)
    for r in range(n // tr):
        rs = pl.ds(r * tr, tr)
        lane = lax.broadcasted_iota(I32, (tr, N_EXPERTS), 1)
        scol = jnp.sum(jnp.where(lane == e, slot_c_ref[0, rs, :].astype(F32), 0.0), axis=1, keepdims=True)
        acol = jnp.sum(jnp.where(lane == e, aff_ref[0, rs, :], 0.0), axis=1, keepdims=True)
        put = jnp.where(scol == lax.broadcasted_iota(I32, (tr, cap), 1).astype(F32), 1.0, 0.0).astype(BF16)
        contrib = (jnp.dot(put, ye_hi, preferred_element_type=F32)
                   + jnp.dot(put, ye_lo, preferred_element_type=F32))
        o_ref[0, rs, :] += acol * contrib


def _moe(h2, slot_t, slot_c, aff, wg, wu, wd, cap):
    b, n, d = h2.shape
    ff = wg.shape[2]
    return pl.pallas_call(
        functools.partial(_moe_kernel, cap=cap),
        grid=(b, N_EXPERTS),
        in_specs=[pl.BlockSpec((1, n, d), lambda i, e: (i, 0, 0)),
                  pl.BlockSpec((1, N_EXPERTS, n), lambda i, e: (i, 0, 0)),
                  pl.BlockSpec((1, n, N_EXPERTS), lambda i, e: (i, 0, 0)),
                  pl.BlockSpec((1, n, N_EXPERTS), lambda i, e: (i, 0, 0)),
                  pl.BlockSpec((1, d, ff), lambda i, e: (e, 0, 0)),
                  pl.BlockSpec((1, d, ff), lambda i, e: (e, 0, 0)),
                  pl.BlockSpec((1, ff, d), lambda i, e: (e, 0, 0))],
        out_specs=pl.BlockSpec((1, n, d), lambda i, e: (i, 0, 0)),
        out_shape=jax.ShapeDtypeStruct((b, n, d), F32),
        name="moe",
        compiler_params=_params(("arbitrary", "arbitrary")),
    )(h2, slot_t, slot_c, aff, wg, wu, wd)


def _resid_kernel(x_ref, y_ref, g_ref, w_ref, o_ref, *, final_norm):
    x = x_ref[0] + g_ref[0] * y_ref[0]
    o_ref[0] = _rms(x, w_ref[...]) if final_norm else x


def _residual(x1, moe, mod, final_w, final_norm):
    b, l, d = x1.shape
    tm = min(l, 512)
    tok = pl.BlockSpec((1, tm, d), lambda i, t: (i, t, 0))
    return pl.pallas_call(
        functools.partial(_resid_kernel, final_norm=final_norm),
        grid=(b, l // tm),
        in_specs=[tok, tok, pl.BlockSpec((1, 1, d), lambda i, t: (i, 0, 5)),
                  pl.BlockSpec((1, d), lambda i, t: (0, 0))],
        out_specs=tok,
        out_shape=jax.ShapeDtypeStruct((b, l, d), F32),
        name="residual",
        compiler_params=_params(("arbitrary", "arbitrary")),
    )(x1, moe, mod, final_w.reshape(1, d))


def _ba_rows(proj):
    b, l, _ = proj.shape
    ba = proj[:, :, COL_BA:COL_BA + 4 * HEADS]
    ba = ba.reshape(b, l // GDN_CHUNK, GDN_CHUNK, 2, 2, HEADS)
    return jnp.transpose(ba, (0, 1, 3, 4, 5, 2))


def _channel_ffn(h2, aff, x1, mod, lw, final_w, final_norm):
    b, n, _ = h2.shape
    cap = EC_CAPACITY * n // N_EXPERTS
    aff16 = aff[:, :, :N_EXPERTS]
    slot_t = _route(jnp.swapaxes(aff16, 1, 2), cap)
    moe = _moe(h2, slot_t, jnp.swapaxes(slot_t, 1, 2), aff16, lw['w_gate'], lw['w_up'], lw['w_down'], cap)
    return _residual(x1, moe, mod, final_w, final_norm)


def kernel(x, c, ctx, c_ctx, w_ada, b_ada, norm1_w, norm2_w, w_in, qkv_conv_w, gdn_a_log, gdn_dt_bias, gdn_norm_w, cmlp_ln_w, cmlp_ln_b, cmlp_w_s, cmlp_b_s, pool_w, pool_scale, w_br_a, w_br_b, w_br_c, w_out, w_router, w_gate, w_up, w_down, final_norm_w):
    b, l, d = x.shape
    depth = w_ada.shape[0]
    assert l % GDN_CHUNK == 0 and ctx.shape[1] % GDN_CHUNK == 0 and l % GRID_W == 0

    cc = jnp.concatenate([c, c_ctx[None, :], jnp.zeros((31 - b, d), F32)], axis=0)
    mods = _modulation(cc, w_ada, b_ada)
    zero_states = jnp.zeros((2, b, HEADS, HEAD_DIM, HEAD_DIM), F32)

    for layer in range(depth):
        last = layer == depth - 1
        mod_x = mods[layer, :b][:, None, :]
        mod_c = jnp.broadcast_to(mods[layer, b][None, None, :], mod_x.shape)

        wi = w_in[layer]
        wp = jnp.concatenate([wi[:, 0:4096], wi[:, 5664:8736], wi[:, 4128:5664], wi[:, 4096:4128],
                              jnp.zeros((d, PROJ_COLS - COL_BA - 4 * HEADS), F32)], axis=1).astype(BF16)
        wr = jnp.concatenate([w_router[layer], jnp.zeros((d, LANES - N_EXPERTS), F32)], axis=1)
        lw = {
            'gdn_norm_w': gdn_norm_w[layer].reshape(1, HEAD_DIM),
            'cmlp_ln_w': cmlp_ln_w[layer].reshape(1, CMLP_WIDTH),
            'cmlp_ln_b': cmlp_ln_b[layer].reshape(1, CMLP_WIDTH),
            'cmlp_w_s': cmlp_w_s[layer].astype(BF16),
            'cmlp_b_s': jnp.broadcast_to(cmlp_b_s[layer][:, :, None],
                                         (CMLP_GROUPS, CMLP_CHUNK, CMLP_WIDTH // CMLP_GROUPS)),
            'pool_w': pool_w[layer].astype(BF16),
            'pool_scale': pool_scale[layer].reshape(1, POOL_WIDTH),
            'w_br_a': w_br_a[layer].astype(BF16), 'w_br_b': w_br_b[layer].astype(BF16),
            'w_br_c': w_br_c[layer].astype(BF16), 'w_out': w_out[layer].astype(BF16),
            'norm2_w': norm2_w[layer].reshape(1, d), 'w_router': wr,
            'w_gate': w_gate[layer].astype(BF16), 'w_up': w_up[layer].astype(BF16),
            'w_down': w_down[layer].astype(BF16),
        }
        alog_b = jnp.broadcast_to(gdn_a_log[layer][:, :, None], (2, HEADS, GDN_CHUNK))
        dtb_b = jnp.broadcast_to(gdn_dt_bias[layer][:, :, None], (2, HEADS, GDN_CHUNK))

        def gdn_path(tokens, mod, states):
            proj = _in_projection(tokens, mod, norm1_w[layer], wp)
            q, k, v = _qkv_conv(proj, qkv_conv_w[layer])
            o, s_fin = _gdn(q, k, v, _ba_rows(proj), alog_b, dtb_b, states)
            return proj, o, s_fin

        proj_c, o_c, ctx_states = gdn_path(ctx, mod_c, zero_states)
        proj_x, o_x, _ = gdn_path(x, mod_x, ctx_states)
        x1, h2, aff = _mix(o_x, proj_x, x, mod_x, lw, GRID_W)
        x = _channel_ffn(h2, aff, x1, mod_x, lw, final_norm_w, last)
        if not last:
            c1, h2c, affc = _mix(o_c, proj_c, ctx, mod_c, lw, ctx.shape[1])
            ctx = _channel_ffn(h2c, affc, c1, mod_c, lw, final_norm_w, False)
    return x
```

```python
import functools

import numpy as np
import jax
import jax.numpy as jnp
from jax import lax
from jax.experimental import pallas as pl
from jax.experimental.pallas import tpu as pltpu

F32 = jnp.float32
BF16 = jnp.bfloat16
I32 = jnp.int32

NORM_EPS = 1e-6
GRID_W = 64
HEADS = 8
HEAD_DIM = 128
GDN_WIDTH = HEADS * HEAD_DIM
QKV_CONV = 5
CMLP_GROUPS = 4
CMLP_CHUNK = 128
CMLP_WIDTH = 512
POOL_WINDOWS = (2, 4, 8, 16)
POOL_WIDTH = 512
N_EXPERTS = 16
EC_CAPACITY = 2

LANES = 128
GDN_CHUNK = 128
GDN_HEADS_PER_STEP = 8
INV_PASSES = 1
VMEM_LIMIT = 56 * 1024 * 1024

COL_Q, COL_K, COL_V, COL_Z = 0, 1024, 2048, 3072
COL_GATES = 4096
COL_U, COL_VG, COL_P = 7168, 7680, 8192
COL_BA = 8704
PROJ_COLS = 8960
PROJ_TN = 1280


def _params(sem, vmem=VMEM_LIMIT):
    return pltpu.CompilerParams(dimension_semantics=sem, vmem_limit_bytes=vmem)


def _dot(a, b):
    return jnp.dot(a.astype(BF16), b.astype(BF16), preferred_element_type=F32)


def _split(a):
    hi = a.astype(BF16)
    lo = (a - hi.astype(F32)).astype(BF16)
    return hi, lo


def _dot_hp(a, b, passes):
    if passes == 1:
        return _dot(a, b)
    a_hi, a_lo = _split(a)
    b_hi, b_lo = _split(b)
    mm = lambda x, y: jnp.dot(x, y, preferred_element_type=F32)
    return mm(a_hi, b_hi) + (mm(a_hi, b_lo) + mm(a_lo, b_hi))


def _split3(a):
    a1 = a.astype(BF16)
    r1 = a - a1.astype(F32)
    a2 = r1.astype(BF16)
    a3 = (r1 - a2.astype(F32)).astype(BF16)
    return a1, a2, a3


def _dot_exact_rhs(a, b_bf16):
    a1, a2, a3 = _split3(a)
    mm = lambda x: jnp.dot(x, b_bf16, preferred_element_type=F32)
    return mm(a1) + (mm(a2) + mm(a3))


def _sigmoid(x):
    return 1.0 / (1.0 + jnp.exp(-x))


def _silu(x):
    return x * _sigmoid(x)


def _gelu_tanh(x):
    return 0.5 * x * (1.0 + jnp.tanh(0.7978845608028654 * (x + 0.044715 * (x * x * x))))


def _softplus(x):
    return jnp.maximum(x, 0.0) + jnp.log(1.0 + jnp.exp(-jnp.abs(x)))


def _rms(x, w):
    return x * lax.rsqrt(jnp.mean(x * x, axis=-1, keepdims=True) + NORM_EPS) * w


def _mod_kernel(c_ref, w_ref, b_ref, o_ref):
    s = _silu(c_ref[...])
    o_ref[0] = _dot_hp(s, w_ref[0], 3) + b_ref[0]


def _modulation(cc, w_ada, b_ada):
    depth, d, n6 = w_ada.shape
    rows = cc.shape[0]
    tn = 1536
    return pl.pallas_call(
        _mod_kernel,
        grid=(depth, n6 // tn),
        in_specs=[pl.BlockSpec((rows, d), lambda l, j: (0, 0)),
                  pl.BlockSpec((1, d, tn), lambda l, j: (l, 0, j)),
                  pl.BlockSpec((1, 1, tn), lambda l, j: (l, 0, j))],
        out_specs=pl.BlockSpec((1, rows, tn), lambda l, j: (l, 0, j)),
        out_shape=jax.ShapeDtypeStruct((depth, rows, n6), F32),
        name="adaln_mod",
        compiler_params=_params(("arbitrary", "arbitrary")),
    )(cc, w_ada, b_ada.reshape(depth, 1, n6))


def _inproj_kernel(x_ref, sh_ref, sc_ref, nw_ref, w_ref, o_ref, ba_ref, h_s):
    j = pl.program_id(2)

    @pl.when(j == 0)
    def _():
        h = _rms(x_ref[0], nw_ref[...]) * (1.0 + sc_ref[0]) + sh_ref[0]
        h_s[...] = h.astype(BF16)

    res = jnp.dot(h_s[...], w_ref[...], preferred_element_type=F32)
    o_ref[0] = res

    @pl.when(j == COL_BA // PROJ_TN)
    def _():
        ba_ref[0] = res[:, COL_BA % PROJ_TN:COL_BA % PROJ_TN + LANES]


def _in_projection(x, mod, norm_w, wp):
    b, l, d = x.shape
    tm = min(l, 1024)
    return pl.pallas_call(
        _inproj_kernel,
        grid=(b, l // tm, PROJ_COLS // PROJ_TN),
        in_specs=[pl.BlockSpec((1, tm, d), lambda i, t, j: (i, t, 0)),
                  pl.BlockSpec((1, 1, d), lambda i, t, j: (i, 0, 0)),
                  pl.BlockSpec((1, 1, d), lambda i, t, j: (i, 0, 1)),
                  pl.BlockSpec((1, d), lambda i, t, j: (0, 0)),
                  pl.BlockSpec((d, PROJ_TN), lambda i, t, j: (0, j))],
        out_specs=[pl.BlockSpec((1, tm, PROJ_TN), lambda i, t, j: (i, t, j)),
                   pl.BlockSpec((1, tm, LANES), lambda i, t, j: (i, t, 0))],
        out_shape=[jax.ShapeDtypeStruct((b, l, PROJ_COLS), F32),
                   jax.ShapeDtypeStruct((b, l, LANES), F32)],
        scratch_shapes=[pltpu.VMEM((tm, d), BF16)],
        name="in_proj",
        compiler_params=_params(("arbitrary", "arbitrary", "arbitrary")),
    )(x, mod, mod, norm_w.reshape(1, d), wp)


CONV_PAD = 8


def _conv_kernel(q_ref, k_ref, v_ref, wq_ref, wk_ref, wv_ref, qo_ref, ko_ref, vo_ref, pad_s):
    l = q_ref.shape[1]
    width = q_ref.shape[2]
    zeros = jnp.zeros((CONV_PAD, width), F32)
    pad_s[pl.ds(0, CONV_PAD), :] = zeros
    pad_s[pl.ds(CONV_PAD + l, CONV_PAD), :] = zeros
    half = QKV_CONV // 2

    def conv_silu(x_ref, w_ref):
        pad_s[pl.ds(CONV_PAD, l), :] = x_ref[0]
        acc = None
        for j in range(QKV_CONV):
            term = pad_s[pl.ds(CONV_PAD - half + j, l), :] * w_ref[pl.ds(j, 1), :]
            acc = term if acc is None else acc + term
        return _silu(acc)

    def l2n(y):
        return y * lax.rsqrt(jnp.sum(y * y, axis=-1, keepdims=True) + NORM_EPS)

    yq = conv_silu(q_ref, wq_ref)
    yk = conv_silu(k_ref, wk_ref)
    yv = conv_silu(v_ref, wv_ref)
    for h in range(width // HEAD_DIM):
        sl = slice(h * HEAD_DIM, (h + 1) * HEAD_DIM)
        qo_ref[0, h] = l2n(yq[:, sl]) * (HEAD_DIM ** -0.5)
        ko_ref[0, h] = l2n(yk[:, sl])
        vo_ref[0, h] = yv[:, sl]


def _qkv_conv(proj, conv_w):
    b, l, _ = proj.shape
    width = 2 * HEAD_DIM
    nq = GDN_WIDTH // width
    hm = jax.ShapeDtypeStruct((b, HEADS, l, HEAD_DIM), F32)
    ospec = pl.BlockSpec((1, width // HEAD_DIM, l, HEAD_DIM), lambda i, j: (i, j, 0, 0))
    return pl.pallas_call(
        _conv_kernel,
        grid=(b, nq),
        in_specs=[pl.BlockSpec((1, l, width), lambda i, j: (i, 0, j)),
                  pl.BlockSpec((1, l, width), lambda i, j: (i, 0, nq + j)),
                  pl.BlockSpec((1, l, width), lambda i, j: (i, 0, 2 * nq + j)),
                  pl.BlockSpec((QKV_CONV, width), lambda i, j: (0, j)),
                  pl.BlockSpec((QKV_CONV, width), lambda i, j: (0, nq + j)),
                  pl.BlockSpec((QKV_CONV, width), lambda i, j: (0, 2 * nq + j))],
        out_specs=[ospec, ospec, ospec],
        out_shape=[hm, hm, hm],
        scratch_shapes=[pltpu.VMEM((l + 2 * CONV_PAD, width), F32)],
        name="qkv_conv",
        compiler_params=_params(("arbitrary", "arbitrary")),
    )(proj, proj, proj, conv_w, conv_w, conv_w)


def _gdn_kernel(qf_ref, kf_ref, vf_ref, qb_ref, kb_ref, vb_ref, baf_ref, bab_ref, alog_ref, dtb_ref,
                s0_ref, of_ref, ob_ref, sfin_ref, state_s, beta_s, gc_s, *, heads_per_step, inv_passes):
    c = GDN_CHUNK
    grp = pl.program_id(1)
    n = pl.program_id(2)

    @pl.when(n == 0)
    def _():
        state_s[...] = s0_ref[:, 0]

    ii = lax.broadcasted_iota(I32, (c, c), 0)
    jj = lax.broadcasted_iota(I32, (c, c), 1)
    xor = ii ^ jj
    eye = jnp.where(ii == jj, 1.0, 0.0)
    levels = c.bit_length() - 1
    later = (ii > jj, ii < jj)
    not_earlier = (ii >= jj, ii <= jj)
    last_idx = (c - 1, 0)
    q_refs, k_refs, v_refs = (qf_ref, qb_ref), (kf_ref, kb_ref), (vf_ref, vb_ref)
    o_refs = (of_ref, ob_ref)

    for d, ba_ref in enumerate((baf_ref, bab_ref)):
        gdec = -jnp.exp(alog_ref[d]) * _softplus(ba_ref[0, 0, 1, 0] + dtb_ref[d])
        cum01 = jnp.where(later[d], 0.0, 1.0).astype(BF16)
        beta_s[d] = _sigmoid(ba_ref[0, 0, 0, 0])
        gc_s[d] = _dot_exact_rhs(gdec, cum01)

    chains = [(d, i) for d in (0, 1) for i in range(heads_per_step)]
    cs = range(len(chains))
    g_rows, bcol, gcol = [], [], []
    for d, i in chains:
        hrow = grp * heads_per_step + i
        b_rows = jnp.broadcast_to(beta_s[d, pl.ds(hrow, 1), :], (c, c))
        g_rows.append(jnp.broadcast_to(gc_s[d, pl.ds(hrow, 1), :], (c, c)))
        bcol.append(b_rows.T)
        gcol.append(g_rows[-1].T)
    q_of = lambda x: q_refs[chains[x][0]][0, chains[x][1]]
    k_of = lambda x: k_refs[chains[x][0]][0, chains[x][1]]
    v_of = lambda x: v_refs[chains[x][0]][0, chains[x][1]]
    dir_of = lambda x: chains[x][0]

    kb = [k_of(x) * bcol[x] for x in cs]
    gram = [lax.dot_general(jnp.concatenate([kb[x], q_of(x)], axis=0).astype(BF16),
                            k_of(x).astype(BF16), (((1,), (1,)), ((), ())),
                            preferred_element_type=F32) for x in cs]
    decay = [jnp.exp(jnp.where(not_earlier[dir_of(x)], gcol[x] - g_rows[x], -jnp.inf)) for x in cs]
    m = [jnp.where(later[dir_of(x)], gram[x][:c] * decay[x], 0.0) for x in cs]
    attn = [gram[x][c:] * decay[x] for x in cs]

    t_inv = [eye - jnp.where(xor == 1, m[x], 0.0) for x in cs]
    for lvl in range(1, levels):
        z = [_dot_hp(jnp.where((xor >> lvl) == 1, m[x], 0.0), t_inv[x], inv_passes) for x in cs]
        t_inv = [t_inv[x] - _dot_hp(t_inv[x], z[x], inv_passes) for x in cs]

    egc = [jnp.exp(gcol[x]) for x in cs]
    uw = [_dot(t_inv[x], jnp.concatenate([v_of(x) * bcol[x], kb[x] * egc[x]], axis=1)) for x in cs]
    s_prev = [state_s[d, i] for d, i in chains]
    ws = [_dot(jnp.concatenate([uw[x][:, HEAD_DIM:], q_of(x) * egc[x]], axis=0), s_prev[x]) for x in cs]
    v_new = [uw[x][:, :HEAD_DIM] - ws[x][:c] for x in cs]
    av = [_dot(attn[x], v_new[x]) for x in cs]
    for x, (d, i) in enumerate(chains):
        o_refs[d][0, i] = ws[x][c:] + av[x]

    g_last = [jnp.broadcast_to(g_rows[x][:, last_idx[dir_of(x)]:last_idx[dir_of(x)] + 1], (c, c)) for x in cs]
    kdec_t = [(k_of(x) * jnp.exp(g_last[x] - gcol[x])).T for x in cs]
    upd = [_dot(kdec_t[x], v_new[x]) for x in cs]
    for x, (d, i) in enumerate(chains):
        state_s[d, i] = s_prev[x] * jnp.exp(g_last[x]) + upd[x]

    @pl.when(n == pl.num_programs(2) - 1)
    def _():
        sfin_ref[:, 0] = state_s[...]


def _gdn(q, k, v, ba_t, alog_b, dtb_b, s0):
    b, h, l, dh = q.shape
    c = GDN_CHUNK
    nch = l // c
    hg = GDN_HEADS_PER_STEP
    fwd = pl.BlockSpec((1, hg, c, dh), lambda i, g, nn: (i, g, nn, 0))
    bwd = pl.BlockSpec((1, hg, c, dh), lambda i, g, nn: (i, g, nch - 1 - nn, 0))
    st_spec = pl.BlockSpec((2, 1, hg, dh, dh), lambda i, g, nn: (0, i, g, 0, 0))
    par_spec = pl.BlockSpec((2, h, c), lambda i, g, nn: (0, 0, 0))
    hm = jax.ShapeDtypeStruct((b, h, l, dh), F32)
    return pl.pallas_call(
        functools.partial(_gdn_kernel, heads_per_step=hg, inv_passes=INV_PASSES),
        grid=(b, h // hg, nch),
        in_specs=[fwd, fwd, fwd, bwd, bwd, bwd,
                  pl.BlockSpec((1, 1, 2, 1, h, c), lambda i, g, nn: (i, nn, 0, 0, 0, 0)),
                  pl.BlockSpec((1, 1, 2, 1, h, c), lambda i, g, nn: (i, nch - 1 - nn, 0, 1, 0, 0)),
                  par_spec, par_spec, st_spec],
        out_specs=[fwd, bwd, st_spec],
        out_shape=[hm, hm, jax.ShapeDtypeStruct((2, b, h, dh, dh), F32)],
        scratch_shapes=[pltpu.VMEM((2, hg, dh, dh), F32),
                        pltpu.VMEM((2, h, c), F32),
                        pltpu.VMEM((2, h, c), F32)],
        name="gdn",
        compiler_params=_params(("arbitrary", "arbitrary", "arbitrary")),
    )(q, k, v, q, k, v, ba_t, ba_t, alog_b, dtb_b, s0)


def _mix_kernel(of_ref, ob_ref, z_ref, ga_ref, gb_ref, gc_ref, u_ref, vg_ref, p_ref, x_ref,
                g1_ref, sh2_ref, sc2_ref, gnw_ref, lnw_ref, lnb_ref, ws_ref, bs_ref,
                pm_ref, icnt_ref, pw_ref, psc_ref, wa_ref, wb_ref, wc_ref, wo_ref,
                n2w_ref, wr_ref, x1_ref, h2_ref, aff_ref):
    tm = x_ref.shape[1]

    z = z_ref[0]
    gnw = gnw_ref[...]
    ya = []
    for h in range(HEADS):
        oh = of_ref[0, h] + ob_ref[0, h]
        zh = z[:, h * HEAD_DIM:(h + 1) * HEAD_DIM]
        ya.append(_rms(oh, gnw) * _silu(zh))
    pa = _dot(jnp.concatenate(ya, axis=1), wa_ref[...])

    u = _gelu_tanh(u_ref[0])
    vg = _gelu_tanh(vg_ref[0])
    mu = jnp.mean(vg, axis=-1, keepdims=True)
    var = jnp.mean(jnp.square(vg - mu), axis=-1, keepdims=True)
    vg = (vg - mu) * lax.rsqrt(var + NORM_EPS) * lnw_ref[...] + lnb_ref[...]
    gd = CMLP_WIDTH // CMLP_GROUPS
    rows = []
    for ch in range(tm // CMLP_CHUNK):
        r = slice(ch * CMLP_CHUNK, (ch + 1) * CMLP_CHUNK)
        cols = [_dot(ws_ref[g], vg[r, g * gd:(g + 1) * gd]) + bs_ref[g] for g in range(CMLP_GROUPS)]
        rows.append(jnp.concatenate(cols, axis=1))
    pb = _dot(u * jnp.concatenate(rows, axis=0), wb_ref[...])

    p = p_ref[0]
    psc = psc_ref[...]
    pg = POOL_WIDTH // len(POOL_WINDOWS)
    yc = []
    for g in range(len(POOL_WINDOWS)):
        pcol = p[:, g * pg:(g + 1) * pg]
        pooled = _dot_exact_rhs_lhs(pm_ref[g], pcol) * icnt_ref[g] - pcol
        yc.append(_dot(pooled, pw_ref[g]) * psc[:, g * pg:(g + 1) * pg])
    pc = _dot(jnp.concatenate(yc, axis=1), wc_ref[...])

    merged = _sigmoid(ga_ref[0]) * pa + _sigmoid(gb_ref[0]) * pb + _sigmoid(gc_ref[0]) * pc
    x1 = x_ref[0] + g1_ref[0] * _dot(merged, wo_ref[...])
    x1_ref[0] = x1

    h2 = _rms(x1, n2w_ref[...]) * (1.0 + sc2_ref[0]) + sh2_ref[0]
    h2_ref[0] = h2.astype(BF16)
    logits = _dot_hp(h2, wr_ref[...], 3)
    lane = lax.broadcasted_iota(I32, logits.shape, 1)
    logits = jnp.where(lane < N_EXPERTS, logits, -jnp.inf)
    e = jnp.exp(logits - jnp.max(logits, axis=-1, keepdims=True))
    aff_ref[0] = e / jnp.sum(e, axis=-1, keepdims=True)


def _dot_exact_rhs_lhs(a_bf16, b):
    b1, b2, b3 = _split3(b)
    mm = lambda y: jnp.dot(a_bf16, y, preferred_element_type=F32)
    return mm(b1) + (mm(b2) + mm(b3))


def _pool_constants(tm, seg):
    t = np.arange(tm)
    pos = t % seg
    base = t - pos
    mats, inv = [], []
    for w in POOL_WINDOWS:
        lo = np.clip(pos - w // 2, 0, seg) + base
        hi = np.clip(pos + w // 2, 0, seg) + base
        s = np.arange(tm)[None, :]
        mats.append(((s >= lo[:, None]) & (s < hi[:, None])).astype(np.float32))
        inv.append(np.broadcast_to((1.0 / (hi - lo).astype(np.float32))[:, None], (tm, POOL_WIDTH // len(POOL_WINDOWS))))
    return jnp.asarray(np.stack(mats), BF16), jnp.asarray(np.stack(inv), F32)


def _mix(o, proj, x, mod, lw, seg):
    b, l, d = x.shape
    tm = min(l, 256)
    pm, icnt = _pool_constants(tm, seg)
    tok = lambda width, blk: pl.BlockSpec((1, tm, width), lambda i, t: (i, t, blk))
    modv = lambda blk: pl.BlockSpec((1, 1, d), lambda i, t: (i, 0, blk))
    full = lambda a: pl.BlockSpec(a.shape, lambda i, t: (0,) * a.ndim)
    consts = [lw['gdn_norm_w'], lw['cmlp_ln_w'], lw['cmlp_ln_b'], lw['cmlp_w_s'], lw['cmlp_b_s'],
              pm, icnt, lw['pool_w'], lw['pool_scale'], lw['w_br_a'], lw['w_br_b'], lw['w_br_c'],
              lw['w_out'], lw['norm2_w'], lw['w_router']]
    return pl.pallas_call(
        _mix_kernel,
        grid=(b, l // tm),
        in_specs=[pl.BlockSpec((1, HEADS, tm, HEAD_DIM), lambda i, t: (i, 0, t, 0)),
                  pl.BlockSpec((1, HEADS, tm, HEAD_DIM), lambda i, t: (i, 0, t, 0)),
                  tok(GDN_WIDTH, COL_Z // GDN_WIDTH),
                  tok(d, COL_GATES // d), tok(d, COL_GATES // d + 1), tok(d, COL_GATES // d + 2),
                  tok(CMLP_WIDTH, COL_U // CMLP_WIDTH), tok(CMLP_WIDTH, COL_VG // CMLP_WIDTH),
                  tok(POOL_WIDTH, COL_P // POOL_WIDTH),
                  tok(d, 0),
                  modv(2), modv(3), modv(4)] + [full(a) for a in consts],
        out_specs=[pl.BlockSpec((1, tm, d), lambda i, t: (i, t, 0)),
                   pl.BlockSpec((1, tm, d), lambda i, t: (i, t, 0)),
                   pl.BlockSpec((1, tm, LANES), lambda i, t: (i, t, 0))],
        out_shape=[jax.ShapeDtypeStruct((b, l, d), F32),
                   jax.ShapeDtypeStruct((b, l, d), BF16),
                   jax.ShapeDtypeStruct((b, l, LANES), F32)],
        name="mix",
        compiler_params=_params(("arbitrary", "arbitrary")),
    )(o[0], o[1], proj, proj, proj, proj, proj, proj, proj, x, mod, mod, mod, *consts)


F32_INF_BITS = 0x7F800000


def _route_kernel(a_ref, u_ref, slot_ref, *, cap):
    bits = pltpu.bitcast(a_ref[0], I32)
    capf = float(cap)

    def body(_, carry):
        lo, hi = carry
        mid = lo + ((hi - lo + 1) >> 1)
        cnt = jnp.sum(jnp.where(bits >= mid, 1.0, 0.0), axis=1, keepdims=True)
        ge = cnt >= capf
        return jnp.where(ge, mid, lo), jnp.where(ge, hi, mid - 1)

    rows = bits.shape[0]
    lo, _ = lax.fori_loop(0, 31, body, (jnp.zeros((rows, 1), I32),
                                        jnp.full((rows, 1), F32_INF_BITS, I32)))
    gt = bits > lo
    eq = bits == lo
    need = capf - jnp.sum(jnp.where(gt, 1.0, 0.0), axis=1, keepdims=True)
    upper = u_ref[...]
    eq_before = jnp.dot(jnp.where(eq, 1.0, 0.0).astype(BF16), upper, preferred_element_type=F32)
    sel = gt | (eq & (eq_before < need))
    pos = jnp.dot(jnp.where(sel, 1.0, 0.0).astype(BF16), upper, preferred_element_type=F32)
    slot_ref[0] = jnp.where(sel, pos.astype(I32), -1)


def _route(aff_t, cap):
    b, e, n = aff_t.shape
    upper = jnp.asarray(np.triu(np.ones((n, n), np.float32), 1), BF16)
    return pl.pallas_call(
        functools.partial(_route_kernel, cap=cap),
        grid=(b,),
        in_specs=[pl.BlockSpec((1, e, n), lambda i: (i, 0, 0)),
                  pl.BlockSpec((n, n), lambda i: (0, 0))],
        out_specs=pl.BlockSpec((1, e, n), lambda i: (i, 0, 0)),
        out_shape=jax.ShapeDtypeStruct((b, e, n), I32),
        name="route",
        compiler_params=_params(("arbitrary",)),
    )(aff_t, upper)


MOE_SCATTER_ROWS = 512
MOE_SET_TOKENS = 2048


def _moe_kernel(h_ref, slot_t_ref, slot_c_ref, aff_ref, wg_ref, wu_ref, wd_ref, o_ref, *, cap):
    e = pl.program_id(1)
    n = h_ref.shape[1]

    @pl.when(e == 0)
    def _():
        o_ref[...] = jnp.zeros_like(o_ref)

    srow = slot_t_ref[0, pl.ds(e, 1), :]
    sel = jnp.where(srow == lax.broadcasted_iota(I32, (cap, n), 0), 1.0, 0.0).astype(BF16)
    xe = jnp.dot(sel, h_ref[0], preferred_element_type=F32).astype(BF16)
    gate = jnp.dot(xe, wg_ref[0], preferred_element_type=F32)
    up = jnp.dot(xe, wu_ref[0], preferred_element_type=F32)
    ye = _dot(_silu(gate) * up, wd_ref[0]).astype(BF16)

    tr = min(n, MOE_SCATTER_ROWS)
    for r in range(n // tr):
        rs = pl.ds(r * tr, tr)
        lane = lax.broadcasted_iota(I32, (tr, N_EXPERTS), 1)
        scol = jnp.sum(jnp.where(lane == e, slot_c_ref[0, rs, :].astype(F32), 0.0), axis=1, keepdims=True)
        acol = jnp.sum(jnp.where(lane == e, aff_ref[0, rs, :], 0.0), axis=1, keepdims=True)
        put = jnp.where(scol == lax.broadcasted_iota(I32, (tr, cap), 1).astype(F32), 1.0, 0.0).astype(BF16)
        o_ref[0, rs, :] += acol * jnp.dot(put, ye, preferred_element_type=F32)


def _moe(h2, slot_t, slot_c, aff, wg, wu, wd, cap):
    g, n, d = h2.shape
    ff = wg.shape[2]
    return pl.pallas_call(
        functools.partial(_moe_kernel, cap=cap),
        grid=(g, N_EXPERTS),
        in_specs=[pl.BlockSpec((1, n, d), lambda i, e: (i, 0, 0)),
                  pl.BlockSpec((1, N_EXPERTS, n), lambda i, e: (i, 0, 0)),
                  pl.BlockSpec((1, n, N_EXPERTS), lambda i, e: (i, 0, 0)),
                  pl.BlockSpec((1, n, N_EXPERTS), lambda i, e: (i, 0, 0)),
                  pl.BlockSpec((1, d, ff), lambda i, e: (e, 0, 0)),
                  pl.BlockSpec((1, d, ff), lambda i, e: (e, 0, 0)),
                  pl.BlockSpec((1, ff, d), lambda i, e: (e, 0, 0))],
        out_specs=pl.BlockSpec((1, n, d), lambda i, e: (i, 0, 0)),
        out_shape=jax.ShapeDtypeStruct((g, n, d), F32),
        name="moe",
        compiler_params=_params(("arbitrary", "arbitrary")),
    )(h2, slot_t, slot_c, aff, wg, wu, wd)


def _resid_kernel(x_ref, y_ref, g_ref, w_ref, o_ref, *, final_norm):
    x = x_ref[0] + g_ref[0] * y_ref[0]
    o_ref[0] = _rms(x, w_ref[...]) if final_norm else x


def _residual(x1, moe, mod, final_w, final_norm, merge):
    b, l, d = x1.shape
    tm = min(l, 512)
    tok = pl.BlockSpec((1, tm, d), lambda i, t: (i, t, 0))
    return pl.pallas_call(
        functools.partial(_resid_kernel, final_norm=final_norm),
        grid=(b, l // tm),
        in_specs=[tok,
                  pl.BlockSpec((1, tm, d), lambda i, t: (i // merge, (i % merge) * (l // tm) + t, 0)),
                  pl.BlockSpec((1, 1, d), lambda i, t: (i, 0, 5)),
                  pl.BlockSpec((1, d), lambda i, t: (0, 0))],
        out_specs=tok,
        out_shape=jax.ShapeDtypeStruct((b, l, d), F32),
        name="residual",
        compiler_params=_params(("arbitrary", "arbitrary")),
    )(x1, moe, mod, final_w.reshape(1, d))


def _ba_rows(ba):
    b, l, _ = ba.shape
    ba = ba[:, :, :4 * HEADS].reshape(b, l // GDN_CHUNK, GDN_CHUNK, 2, 2, HEADS)
    return jnp.transpose(ba, (0, 1, 3, 4, 5, 2))


def _channel_ffn(h2, aff, x1, mod, lw, final_w, final_norm):
    b, n, d = h2.shape
    cap = EC_CAPACITY * n // N_EXPERTS
    aff_t = jnp.swapaxes(aff[:, :, :N_EXPERTS], 1, 2)
    slot_t = _route(aff_t, cap)
    merge = max(1, min(b, MOE_SET_TOKENS // n))
    while b % merge:
        merge -= 1
    if merge > 1:
        offset = (jnp.arange(b, dtype=I32) % merge * cap)[:, None, None]
        slot_t = jnp.where(slot_t >= 0, slot_t + offset, -1)
        join = lambda a: jnp.swapaxes(a.reshape(b // merge, merge, N_EXPERTS, n), 1, 2).reshape(
            b // merge, N_EXPERTS, merge * n)
        slot_t = join(slot_t)
    sets = lambda a: a.reshape(b // merge, merge * n, a.shape[-1])
    moe = _moe(sets(h2), slot_t, jnp.swapaxes(slot_t, 1, 2), sets(aff[:, :, :N_EXPERTS]),
               lw['w_gate'], lw['w_up'], lw['w_down'], cap * merge)
    return _residual(x1, moe, mod, final_w, final_norm, merge)


def kernel(x, c, ctx, c_ctx, w_ada, b_ada, norm1_w, norm2_w, w_in, qkv_conv_w, gdn_a_log, gdn_dt_bias, gdn_norm_w, cmlp_ln_w, cmlp_ln_b, cmlp_w_s, cmlp_b_s, pool_w, pool_scale, w_br_a, w_br_b, w_br_c, w_out, w_router, w_gate, w_up, w_down, final_norm_w):
    b, l, d = x.shape
    depth = w_ada.shape[0]
    assert l % GDN_CHUNK == 0 and ctx.shape[1] % GDN_CHUNK == 0 and l % GRID_W == 0

    cc = jnp.concatenate([c, c_ctx[None, :], jnp.zeros((31 - b, d), F32)], axis=0)
    mods = _modulation(cc, w_ada, b_ada)
    zero_states = jnp.zeros((2, b, HEADS, HEAD_DIM, HEAD_DIM), F32)

    for layer in range(depth):
        last = layer == depth - 1
        mod_x = mods[layer, :b][:, None, :]
        mod_c = jnp.broadcast_to(mods[layer, b][None, None, :], mod_x.shape)

        wi = w_in[layer]
        wp = jnp.concatenate([wi[:, 0:4096], wi[:, 5664:8736], wi[:, 4128:5664], wi[:, 4096:4128],
                              jnp.zeros((d, PROJ_COLS - COL_BA - 4 * HEADS), F32)], axis=1).astype(BF16)
        wr = jnp.concatenate([w_router[layer], jnp.zeros((d, LANES - N_EXPERTS), F32)], axis=1)
        lw = {
            'gdn_norm_w': gdn_norm_w[layer].reshape(1, HEAD_DIM),
            'cmlp_ln_w': cmlp_ln_w[layer].reshape(1, CMLP_WIDTH),
            'cmlp_ln_b': cmlp_ln_b[layer].reshape(1, CMLP_WIDTH),
            'cmlp_w_s': cmlp_w_s[layer].astype(BF16),
            'cmlp_b_s': jnp.broadcast_to(cmlp_b_s[layer][:, :, None],
                                         (CMLP_GROUPS, CMLP_CHUNK, CMLP_WIDTH // CMLP_GROUPS)),
            'pool_w': pool_w[layer].astype(BF16),
            'pool_scale': pool_scale[layer].reshape(1, POOL_WIDTH),
            'w_br_a': w_br_a[layer].astype(BF16), 'w_br_b': w_br_b[layer].astype(BF16),
            'w_br_c': w_br_c[layer].astype(BF16), 'w_out': w_out[layer].astype(BF16),
            'norm2_w': norm2_w[layer].reshape(1, d), 'w_router': wr,
            'w_gate': w_gate[layer].astype(BF16), 'w_up': w_up[layer].astype(BF16),
            'w_down': w_down[layer].astype(BF16),
        }
        alog_b = jnp.broadcast_to(gdn_a_log[layer][:, :, None], (2, HEADS, GDN_CHUNK))
        dtb_b = jnp.broadcast_to(gdn_dt_bias[layer][:, :, None], (2, HEADS, GDN_CHUNK))

        def gdn_path(tokens, mod, states, shared_mod):
            nb, nl, _ = tokens.shape
            if shared_mod:
                proj, ba = _in_projection(tokens.reshape(1, nb * nl, d), mod[:1], norm1_w[layer], wp)
                proj, ba = proj.reshape(nb, nl, PROJ_COLS), ba.reshape(nb, nl, LANES)
            else:
                proj, ba = _in_projection(tokens, mod, norm1_w[layer], wp)
            q, k, v = _qkv_conv(proj, qkv_conv_w[layer])
            o_f, o_b, s_fin = _gdn(q, k, v, _ba_rows(ba), alog_b, dtb_b, states)
            return proj, (o_f, o_b), s_fin

        proj_c, o_c, ctx_states = gdn_path(ctx, mod_c, zero_states, True)
        proj_x, o_x, _ = gdn_path(x, mod_x, ctx_states, False)
        x1, h2, aff = _mix(o_x, proj_x, x, mod_x, lw, GRID_W)
        x = _channel_ffn(h2, aff, x1, mod_x, lw, final_norm_w, last)
        if not last:
            c1, h2c, affc = _mix(o_c, proj_c, ctx, mod_c, lw, ctx.shape[1])
            ctx = _channel_ffn(h2c, affc, c1, mod_c, lw, final_norm_w, False)
    return x
```

```python
import functools

import numpy as np
import jax
import jax.numpy as jnp
from jax import lax
from jax.experimental import pallas as pl
from jax.experimental.pallas import tpu as pltpu

F32 = jnp.float32
BF16 = jnp.bfloat16
I32 = jnp.int32

NORM_EPS = 1e-6
GRID_W = 64
HEADS = 8
HEAD_DIM = 128
GDN_WIDTH = HEADS * HEAD_DIM
QKV_CONV = 5
CMLP_GROUPS = 4
CMLP_CHUNK = 128
CMLP_WIDTH = 512
POOL_WINDOWS = (2, 4, 8, 16)
POOL_WIDTH = 512
N_EXPERTS = 16
EC_CAPACITY = 2

LANES = 128
GDN_CHUNK = 128
GDN_HEADS_PER_STEP = 8
INV_PASSES = 1
VMEM_LIMIT = 56 * 1024 * 1024

COL_Q, COL_K, COL_V, COL_Z = 0, 1024, 2048, 3072
COL_GATES = 4096
COL_U, COL_VG, COL_P = 7168, 7680, 8192
COL_BA = 8704
PROJ_COLS = 8960
PROJ_TN = 1280
INPROJ_SET_TOKENS = 2048


def _params(sem, vmem=VMEM_LIMIT):
    return pltpu.CompilerParams(dimension_semantics=sem, vmem_limit_bytes=vmem)


def _dot(a, b):
    return jnp.dot(a.astype(BF16), b.astype(BF16), preferred_element_type=F32)


def _split(a):
    hi = a.astype(BF16)
    lo = (a - hi.astype(F32)).astype(BF16)
    return hi, lo


def _dot_hp(a, b, passes):
    if passes == 1:
        return _dot(a, b)
    a_hi, a_lo = _split(a)
    b_hi, b_lo = _split(b)
    mm = lambda x, y: jnp.dot(x, y, preferred_element_type=F32)
    return mm(a_hi, b_hi) + (mm(a_hi, b_lo) + mm(a_lo, b_hi))


def _split3(a):
    a1 = a.astype(BF16)
    r1 = a - a1.astype(F32)
    a2 = r1.astype(BF16)
    a3 = (r1 - a2.astype(F32)).astype(BF16)
    return a1, a2, a3


def _dot_exact_rhs(a, b_bf16):
    a1, a2, a3 = _split3(a)
    mm = lambda x: jnp.dot(x, b_bf16, preferred_element_type=F32)
    return mm(a1) + (mm(a2) + mm(a3))


def _sigmoid(x):
    return 1.0 / (1.0 + jnp.exp(-x))


def _silu(x):
    return x * _sigmoid(x)


def _gelu_tanh(x):
    return 0.5 * x * (1.0 + jnp.tanh(0.7978845608028654 * (x + 0.044715 * (x * x * x))))


def _softplus(x):
    return jnp.maximum(x, 0.0) + jnp.log(1.0 + jnp.exp(-jnp.abs(x)))


def _rms(x, w):
    return x * lax.rsqrt(jnp.mean(x * x, axis=-1, keepdims=True) + NORM_EPS) * w


def _mod_kernel(c_ref, w_ref, b_ref, o_ref):
    s = _silu(c_ref[...])
    o_ref[0] = _dot_hp(s, w_ref[0], 3) + b_ref[0]


def _modulation(cc, w_ada, b_ada):
    depth, d, n6 = w_ada.shape
    rows = cc.shape[0]
    tn = 1536
    return pl.pallas_call(
        _mod_kernel,
        grid=(depth, n6 // tn),
        in_specs=[pl.BlockSpec((rows, d), lambda l, j: (0, 0)),
                  pl.BlockSpec((1, d, tn), lambda l, j: (l, 0, j)),
                  pl.BlockSpec((1, 1, tn), lambda l, j: (l, 0, j))],
        out_specs=pl.BlockSpec((1, rows, tn), lambda l, j: (l, 0, j)),
        out_shape=jax.ShapeDtypeStruct((depth, rows, n6), F32),
        name="adaln_mod",
        compiler_params=_params(("arbitrary", "arbitrary")),
    )(cc, w_ada, b_ada.reshape(depth, 1, n6))


def _inproj_kernel(x_ref, sh_ref, sc_ref, nw_ref, w_ref, o_ref, h_s):
    t = pl.program_id(2)
    tm = o_ref.shape[1]

    @pl.when(pl.program_id(1) == 0)
    def _():
        rows = pl.ds(pl.multiple_of(t * tm, tm), tm)
        h = _rms(x_ref[0, rows, :], nw_ref[...]) * (1.0 + sc_ref[0]) + sh_ref[0]
        h_s[t] = h.astype(BF16)

    o_ref[0] = jnp.dot(h_s[t], w_ref[...], preferred_element_type=F32)


def _in_projection(x, mod, norm_w, wp):
    b, l, d = x.shape
    tm = min(l, 1024)
    return pl.pallas_call(
        _inproj_kernel,
        grid=(b, PROJ_COLS // PROJ_TN, l // tm),
        in_specs=[pl.BlockSpec((1, l, d), lambda i, j, t: (i, 0, 0)),
                  pl.BlockSpec((1, 1, d), lambda i, j, t: (i, 0, 0)),
                  pl.BlockSpec((1, 1, d), lambda i, j, t: (i, 0, 1)),
                  pl.BlockSpec((1, d), lambda i, j, t: (0, 0)),
                  pl.BlockSpec((d, PROJ_TN), lambda i, j, t: (0, j))],
        out_specs=pl.BlockSpec((1, tm, PROJ_TN), lambda i, j, t: (i, t, j)),
        out_shape=jax.ShapeDtypeStruct((b, l, PROJ_COLS), F32),
        scratch_shapes=[pltpu.VMEM((l // tm, tm, d), BF16)],
        name="in_proj",
        compiler_params=_params(("arbitrary", "arbitrary", "arbitrary")),
    )(x, mod, mod, norm_w.reshape(1, d), wp)


CONV_PAD = 8


def _conv_kernel(q_ref, k_ref, v_ref, wq_ref, wk_ref, wv_ref, qo_ref, ko_ref, vo_ref, pad_s):
    l = q_ref.shape[1]
    width = q_ref.shape[2]
    zeros = jnp.zeros((CONV_PAD, width), F32)
    pad_s[pl.ds(0, CONV_PAD), :] = zeros
    pad_s[pl.ds(CONV_PAD + l, CONV_PAD), :] = zeros
    half = QKV_CONV // 2

    def conv_silu(x_ref, w_ref):
        pad_s[pl.ds(CONV_PAD, l), :] = x_ref[0]
        acc = None
        for j in range(QKV_CONV):
            term = pad_s[pl.ds(CONV_PAD - half + j, l), :] * w_ref[pl.ds(j, 1), :]
            acc = term if acc is None else acc + term
        return _silu(acc)

    def l2n(y):
        return y * lax.rsqrt(jnp.sum(y * y, axis=-1, keepdims=True) + NORM_EPS)

    yq = conv_silu(q_ref, wq_ref)
    yk = conv_silu(k_ref, wk_ref)
    yv = conv_silu(v_ref, wv_ref)
    for h in range(width // HEAD_DIM):
        sl = slice(h * HEAD_DIM, (h + 1) * HEAD_DIM)
        qo_ref[0, h] = l2n(yq[:, sl]) * (HEAD_DIM ** -0.5)
        ko_ref[0, h] = l2n(yk[:, sl])
        vo_ref[0, h] = yv[:, sl]


def _qkv_conv(proj, conv_w):
    b, l, _ = proj.shape
    width = 2 * HEAD_DIM
    nq = GDN_WIDTH // width
    hm = jax.ShapeDtypeStruct((b, HEADS, l, HEAD_DIM), F32)
    ospec = pl.BlockSpec((1, width // HEAD_DIM, l, HEAD_DIM), lambda i, j: (i, j, 0, 0))
    return pl.pallas_call(
        _conv_kernel,
        grid=(b, nq),
        in_specs=[pl.BlockSpec((1, l, width), lambda i, j: (i, 0, j)),
                  pl.BlockSpec((1, l, width), lambda i, j: (i, 0, nq + j)),
                  pl.BlockSpec((1, l, width), lambda i, j: (i, 0, 2 * nq + j)),
                  pl.BlockSpec((QKV_CONV, width), lambda i, j: (0, j)),
                  pl.BlockSpec((QKV_CONV, width), lambda i, j: (0, nq + j)),
                  pl.BlockSpec((QKV_CONV, width), lambda i, j: (0, 2 * nq + j))],
        out_specs=[ospec, ospec, ospec],
        out_shape=[hm, hm, hm],
        scratch_shapes=[pltpu.VMEM((l + 2 * CONV_PAD, width), F32)],
        name="qkv_conv",
        compiler_params=_params(("arbitrary", "arbitrary")),
    )(proj, proj, proj, conv_w, conv_w, conv_w)


def _gdn_kernel(qf_ref, kf_ref, vf_ref, qb_ref, kb_ref, vb_ref, baf_ref, bab_ref, alog_ref, dtb_ref,
                s0_ref, of_ref, ob_ref, sfin_ref, state_s, beta_s, gc_s, *, heads_per_step, inv_passes):
    c = GDN_CHUNK
    grp = pl.program_id(1)
    n = pl.program_id(2)

    @pl.when(n == 0)
    def _():
        state_s[...] = s0_ref[:, 0]

    ii = lax.broadcasted_iota(I32, (c, c), 0)
    jj = lax.broadcasted_iota(I32, (c, c), 1)
    xor = ii ^ jj
    eye = jnp.where(ii == jj, 1.0, 0.0)
    levels = c.bit_length() - 1
    later = (ii > jj, ii < jj)
    not_earlier = (ii >= jj, ii <= jj)
    last_idx = (c - 1, 0)
    q_refs, k_refs, v_refs = (qf_ref, qb_ref), (kf_ref, kb_ref), (vf_ref, vb_ref)
    o_refs = (of_ref, ob_ref)

    for d, ba_ref in enumerate((baf_ref, bab_ref)):
        gdec = -jnp.exp(alog_ref[d]) * _softplus(ba_ref[0, 0, 1, 0] + dtb_ref[d])
        cum01 = jnp.where(later[d], 0.0, 1.0).astype(BF16)
        beta_s[d] = _sigmoid(ba_ref[0, 0, 0, 0])
        gc_s[d] = _dot_exact_rhs(gdec, cum01)

    chains = [(d, i) for d in (0, 1) for i in range(heads_per_step)]
    cs = range(len(chains))
    g_rows, bcol, gcol = [], [], []
    for d, i in chains:
        hrow = grp * heads_per_step + i
        b_rows = jnp.broadcast_to(beta_s[d, pl.ds(hrow, 1), :], (c, c))
        g_rows.append(jnp.broadcast_to(gc_s[d, pl.ds(hrow, 1), :], (c, c)))
        bcol.append(b_rows.T)
        gcol.append(g_rows[-1].T)
    q_of = lambda x: q_refs[chains[x][0]][0, chains[x][1]]
    k_of = lambda x: k_refs[chains[x][0]][0, chains[x][1]]
    v_of = lambda x: v_refs[chains[x][0]][0, chains[x][1]]
    dir_of = lambda x: chains[x][0]

    kb = [k_of(x) * bcol[x] for x in cs]
    gram = [lax.dot_general(jnp.concatenate([kb[x], q_of(x)], axis=0).astype(BF16),
                            k_of(x).astype(BF16), (((1,), (1,)), ((), ())),
                            preferred_element_type=F32) for x in cs]
    decay = [jnp.exp(jnp.where(not_earlier[dir_of(x)], gcol[x] - g_rows[x], -jnp.inf)) for x in cs]
    m = [jnp.where(later[dir_of(x)], gram[x][:c] * decay[x], 0.0) for x in cs]
    attn = [gram[x][c:] * decay[x] for x in cs]

    t_inv = [eye - jnp.where(xor == 1, m[x], 0.0) for x in cs]
    for lvl in range(1, levels):
        z = [_dot_hp(jnp.where((xor >> lvl) == 1, m[x], 0.0), t_inv[x], inv_passes) for x in cs]
        t_inv = [t_inv[x] - _dot_hp(t_inv[x], z[x], inv_passes) for x in cs]

    egc = [jnp.exp(gcol[x]) for x in cs]
    uw = [_dot(t_inv[x], jnp.concatenate([v_of(x) * bcol[x], kb[x] * egc[x]], axis=1)) for x in cs]
    s_prev = [state_s[d, i] for d, i in chains]
    ws = [_dot(jnp.concatenate([uw[x][:, HEAD_DIM:], q_of(x) * egc[x]], axis=0), s_prev[x]) for x in cs]
    v_new = [uw[x][:, :HEAD_DIM] - ws[x][:c] for x in cs]
    av = [_dot(attn[x], v_new[x]) for x in cs]
    for x, (d, i) in enumerate(chains):
        o_refs[d][0, i] = ws[x][c:] + av[x]

    g_last = [jnp.broadcast_to(g_rows[x][:, last_idx[dir_of(x)]:last_idx[dir_of(x)] + 1], (c, c)) for x in cs]
    kdec_t = [(k_of(x) * jnp.exp(g_last[x] - gcol[x])).T for x in cs]
    upd = [_dot(kdec_t[x], v_new[x]) for x in cs]
    for x, (d, i) in enumerate(chains):
        state_s[d, i] = s_prev[x] * jnp.exp(g_last[x]) + upd[x]

    @pl.when(n == pl.num_programs(2) - 1)
    def _():
        sfin_ref[:, 0] = state_s[...]


def _gdn(q, k, v, ba_t, alog_b, dtb_b, s0):
    b, h, l, dh = q.shape
    c = GDN_CHUNK
    nch = l // c
    hg = GDN_HEADS_PER_STEP
    fwd = pl.BlockSpec((1, hg, c, dh), lambda i, g, nn: (i, g, nn, 0))
    bwd = pl.BlockSpec((1, hg, c, dh), lambda i, g, nn: (i, g, nch - 1 - nn, 0))
    st_spec = pl.BlockSpec((2, 1, hg, dh, dh), lambda i, g, nn: (0, i, g, 0, 0))
    par_spec = pl.BlockSpec((2, h, c), lambda i, g, nn: (0, 0, 0))
    hm = jax.ShapeDtypeStruct((b, h, l, dh), F32)
    return pl.pallas_call(
        functools.partial(_gdn_kernel, heads_per_step=hg, inv_passes=INV_PASSES),
        grid=(b, h // hg, nch),
        in_specs=[fwd, fwd, fwd, bwd, bwd, bwd,
                  pl.BlockSpec((1, 1, 2, 1, h, c), lambda i, g, nn: (i, nn, 0, 0, 0, 0)),
                  pl.BlockSpec((1, 1, 2, 1, h, c), lambda i, g, nn: (i, nch - 1 - nn, 0, 1, 0, 0)),
                  par_spec, par_spec, st_spec],
        out_specs=[fwd, bwd, st_spec],
        out_shape=[hm, hm, jax.ShapeDtypeStruct((2, b, h, dh, dh), F32)],
        scratch_shapes=[pltpu.VMEM((2, hg, dh, dh), F32),
                        pltpu.VMEM((2, h, c), F32),
                        pltpu.VMEM((2, h, c), F32)],
        name="gdn",
        compiler_params=_params(("arbitrary", "arbitrary", "arbitrary")),
    )(q, k, v, q, k, v, ba_t, ba_t, alog_b, dtb_b, s0)


def _mix_kernel(of_ref, ob_ref, z_ref, ga_ref, gb_ref, gc_ref, u_ref, vg_ref, p_ref, x_ref,
                g1_ref, sh2_ref, sc2_ref, gnw_ref, lnw_ref, lnb_ref, ws_ref, bs_ref,
                pm_ref, icnt_ref, pw_ref, psc_ref, wa_ref, wb_ref, wc_ref, wo_ref,
                n2w_ref, wr_ref, x1_ref, h2_ref, aff_ref):
    tm = x_ref.shape[1]

    z = z_ref[0]
    gnw = gnw_ref[...]
    ya = []
    for h in range(HEADS):
        oh = of_ref[0, h] + ob_ref[0, h]
        zh = z[:, h * HEAD_DIM:(h + 1) * HEAD_DIM]
        ya.append(_rms(oh, gnw) * _silu(zh))
    pa = _dot(jnp.concatenate(ya, axis=1), wa_ref[...])

    u = _gelu_tanh(u_ref[0])
    vg = _gelu_tanh(vg_ref[0])
    mu = jnp.mean(vg, axis=-1, keepdims=True)
    var = jnp.mean(jnp.square(vg - mu), axis=-1, keepdims=True)
    vg = (vg - mu) * lax.rsqrt(var + NORM_EPS) * lnw_ref[...] + lnb_ref[...]
    gd = CMLP_WIDTH // CMLP_GROUPS
    rows = []
    for ch in range(tm // CMLP_CHUNK):
        r = slice(ch * CMLP_CHUNK, (ch + 1) * CMLP_CHUNK)
        cols = [_dot(ws_ref[g], vg[r, g * gd:(g + 1) * gd]) + bs_ref[g] for g in range(CMLP_GROUPS)]
        rows.append(jnp.concatenate(cols, axis=1))
    pb = _dot(u * jnp.concatenate(rows, axis=0), wb_ref[...])

    p = p_ref[0]
    psc = psc_ref[...]
    pg = POOL_WIDTH // len(POOL_WINDOWS)
    yc = []
    for g in range(len(POOL_WINDOWS)):
        pcol = p[:, g * pg:(g + 1) * pg]
        pooled = _dot_exact_rhs_lhs(pm_ref[g], pcol) * icnt_ref[g] - pcol
        yc.append(_dot(pooled, pw_ref[g]) * psc[:, g * pg:(g + 1) * pg])
    pc = _dot(jnp.concatenate(yc, axis=1), wc_ref[...])

    merged = _sigmoid(ga_ref[0]) * pa + _sigmoid(gb_ref[0]) * pb + _sigmoid(gc_ref[0]) * pc
    x1 = x_ref[0] + g1_ref[0] * _dot(merged, wo_ref[...])
    x1_ref[0] = x1

    h2 = _rms(x1, n2w_ref[...]) * (1.0 + sc2_ref[0]) + sh2_ref[0]
    h2_ref[0] = h2.astype(BF16)
    logits = _dot_hp(h2, wr_ref[...], 3)
    lane = lax.broadcasted_iota(I32, logits.shape, 1)
    logits = jnp.where(lane < N_EXPERTS, logits, -jnp.inf)
    e = jnp.exp(logits - jnp.max(logits, axis=-1, keepdims=True))
    aff_ref[0] = e / jnp.sum(e, axis=-1, keepdims=True)


def _dot_exact_rhs_lhs(a_bf16, b):
    b1, b2, b3 = _split3(b)
    mm = lambda y: jnp.dot(a_bf16, y, preferred_element_type=F32)
    return mm(b1) + (mm(b2) + mm(b3))


def _pool_constants(tm, seg):
    t = np.arange(tm)
    pos = t % seg
    base = t - pos
    mats, inv = [], []
    for w in POOL_WINDOWS:
        lo = np.clip(pos - w // 2, 0, seg) + base
        hi = np.clip(pos + w // 2, 0, seg) + base
        s = np.arange(tm)[None, :]
        mats.append(((s >= lo[:, None]) & (s < hi[:, None])).astype(np.float32))
        inv.append(np.broadcast_to((1.0 / (hi - lo).astype(np.float32))[:, None], (tm, POOL_WIDTH // len(POOL_WINDOWS))))
    return jnp.asarray(np.stack(mats), BF16), jnp.asarray(np.stack(inv), F32)


def _mix(o, proj, x, mod, lw, seg):
    b, l, d = x.shape
    tm = min(l, 256)
    pm, icnt = _pool_constants(tm, seg)
    tok = lambda width, blk: pl.BlockSpec((1, tm, width), lambda i, t: (i, t, blk))
    modv = lambda blk: pl.BlockSpec((1, 1, d), lambda i, t: (i, 0, blk))
    full = lambda a: pl.BlockSpec(a.shape, lambda i, t: (0,) * a.ndim)
    consts = [lw['gdn_norm_w'], lw['cmlp_ln_w'], lw['cmlp_ln_b'], lw['cmlp_w_s'], lw['cmlp_b_s'],
              pm, icnt, lw['pool_w'], lw['pool_scale'], lw['w_br_a'], lw['w_br_b'], lw['w_br_c'],
              lw['w_out'], lw['norm2_w'], lw['w_router']]
    return pl.pallas_call(
        _mix_kernel,
        grid=(b, l // tm),
        in_specs=[pl.BlockSpec((1, HEADS, tm, HEAD_DIM), lambda i, t: (i, 0, t, 0)),
                  pl.BlockSpec((1, HEADS, tm, HEAD_DIM), lambda i, t: (i, 0, t, 0)),
                  tok(GDN_WIDTH, COL_Z // GDN_WIDTH),
                  tok(d, COL_GATES // d), tok(d, COL_GATES // d + 1), tok(d, COL_GATES // d + 2),
                  tok(CMLP_WIDTH, COL_U // CMLP_WIDTH), tok(CMLP_WIDTH, COL_VG // CMLP_WIDTH),
                  tok(POOL_WIDTH, COL_P // POOL_WIDTH),
                  tok(d, 0),
                  modv(2), modv(3), modv(4)] + [full(a) for a in consts],
        out_specs=[pl.BlockSpec((1, tm, d), lambda i, t: (i, t, 0)),
                   pl.BlockSpec((1, tm, d), lambda i, t: (i, t, 0)),
                   pl.BlockSpec((1, tm, LANES), lambda i, t: (i, t, 0))],
        out_shape=[jax.ShapeDtypeStruct((b, l, d), F32),
                   jax.ShapeDtypeStruct((b, l, d), BF16),
                   jax.ShapeDtypeStruct((b, l, LANES), F32)],
        name="mix",
        compiler_params=_params(("arbitrary", "arbitrary")),
    )(o[0], o[1], proj, proj, proj, proj, proj, proj, proj, x, mod, mod, mod, *consts)


F32_INF_BITS = 0x7F800000


ROUTE_ROWS = 256


def _route_kernel(a_ref, u_ref, slot_ref, *, cap):
    bits = pltpu.bitcast(a_ref[...], I32)
    capf = float(cap)

    def body(_, carry):
        lo, hi = carry
        mid = lo + ((hi - lo + 1) >> 1)
        cnt = jnp.sum(jnp.where(bits >= mid, 1.0, 0.0), axis=1, keepdims=True)
        ge = cnt >= capf
        return jnp.where(ge, mid, lo), jnp.where(ge, hi, mid - 1)

    rows = bits.shape[0]
    lo, _ = lax.fori_loop(0, 31, body, (jnp.zeros((rows, 1), I32),
                                        jnp.full((rows, 1), F32_INF_BITS, I32)))
    gt = bits > lo
    eq = bits == lo
    need = capf - jnp.sum(jnp.where(gt, 1.0, 0.0), axis=1, keepdims=True)
    upper = u_ref[...]
    eq_before = jnp.dot(jnp.where(eq, 1.0, 0.0).astype(BF16), upper, preferred_element_type=F32)
    sel = gt | (eq & (eq_before < need))
    pos = jnp.dot(jnp.where(sel, 1.0, 0.0).astype(BF16), upper, preferred_element_type=F32)
    slot_ref[...] = jnp.where(sel, pos.astype(I32), -1)


def _route(aff_t, cap):
    b, e, n = aff_t.shape
    rows = min(b * e, ROUTE_ROWS)
    upper = jnp.asarray(np.triu(np.ones((n, n), np.float32), 1), BF16)
    slots = pl.pallas_call(
        functools.partial(_route_kernel, cap=cap),
        grid=(b * e // rows,),
        in_specs=[pl.BlockSpec((rows, n), lambda i: (i, 0)),
                  pl.BlockSpec((n, n), lambda i: (0, 0))],
        out_specs=pl.BlockSpec((rows, n), lambda i: (i, 0)),
        out_shape=jax.ShapeDtypeStruct((b * e, n), I32),
        name="route",
        compiler_params=_params(("arbitrary",)),
    )(aff_t.reshape(b * e, n), upper)
    return slots.reshape(b, e, n)


MOE_SCATTER_ROWS = 512
MOE_SET_TOKENS = 2048


def _moe_kernel(h_ref, slot_t_ref, slot_c_ref, aff_ref, wg_ref, wu_ref, wd_ref, o_ref, *, cap):
    e = pl.program_id(1)
    n = h_ref.shape[1]

    @pl.when(e == 0)
    def _():
        o_ref[...] = jnp.zeros_like(o_ref)

    srow = slot_t_ref[0, pl.ds(e, 1), :]
    sel = jnp.where(srow == lax.broadcasted_iota(I32, (cap, n), 0), 1.0, 0.0).astype(BF16)
    xe = jnp.dot(sel, h_ref[0], preferred_element_type=F32).astype(BF16)
    gate = jnp.dot(xe, wg_ref[0, 0], preferred_element_type=F32)
    up = jnp.dot(xe, wu_ref[0, 0], preferred_element_type=F32)
    ye = _dot(_silu(gate) * up, wd_ref[0, 0]).astype(BF16)

    tr = min(n, MOE_SCATTER_ROWS)
    for r in range(n // tr):
        rs = pl.ds(r * tr, tr)
        lane = lax.broadcasted_iota(I32, (tr, N_EXPERTS), 1)
        scol = jnp.sum(jnp.where(lane == e, slot_c_ref[0, rs, :].astype(F32), 0.0), axis=1, keepdims=True)
        acol = jnp.sum(jnp.where(lane == e, aff_ref[0, rs, :], 0.0), axis=1, keepdims=True)
        put = jnp.where(scol == lax.broadcasted_iota(I32, (tr, cap), 1).astype(F32), 1.0, 0.0).astype(BF16)
        o_ref[0, rs, :] += acol * jnp.dot(put, ye, preferred_element_type=F32)


def _moe(h2, slot_t, slot_c, aff, wg, wu, wd, layer, cap):
    g, n, d = h2.shape
    ff = wg.shape[3]
    return pl.pallas_call(
        functools.partial(_moe_kernel, cap=cap),
        grid=(g, N_EXPERTS),
        in_specs=[pl.BlockSpec((1, n, d), lambda i, e: (i, 0, 0)),
                  pl.BlockSpec((1, N_EXPERTS, n), lambda i, e: (i, 0, 0)),
                  pl.BlockSpec((1, n, N_EXPERTS), lambda i, e: (i, 0, 0)),
                  pl.BlockSpec((1, n, N_EXPERTS), lambda i, e: (i, 0, 0)),
                  pl.BlockSpec((1, 1, d, ff), lambda i, e: (layer, e, 0, 0)),
                  pl.BlockSpec((1, 1, d, ff), lambda i, e: (layer, e, 0, 0)),
                  pl.BlockSpec((1, 1, ff, d), lambda i, e: (layer, e, 0, 0))],
        out_specs=pl.BlockSpec((1, n, d), lambda i, e: (i, 0, 0)),
        out_shape=jax.ShapeDtypeStruct((g, n, d), F32),
        name="moe",
        compiler_params=_params(("arbitrary", "arbitrary")),
    )(h2, slot_t, slot_c, aff, wg, wu, wd)


def _resid_kernel(x_ref, y_ref, g_ref, w_ref, o_ref, *, final_norm):
    x = x_ref[0] + g_ref[0] * y_ref[0]
    o_ref[0] = _rms(x, w_ref[...]) if final_norm else x


def _residual(x1, moe, mod, final_w, final_norm, merge):
    b, l, d = x1.shape
    tm = min(l, 512)
    tok = pl.BlockSpec((1, tm, d), lambda i, t: (i, t, 0))
    return pl.pallas_call(
        functools.partial(_resid_kernel, final_norm=final_norm),
        grid=(b, l // tm),
        in_specs=[tok,
                  pl.BlockSpec((1, tm, d), lambda i, t: (i // merge, (i % merge) * (l // tm) + t, 0)),
                  pl.BlockSpec((1, 1, d), lambda i, t: (i, 0, 5)),
                  pl.BlockSpec((1, d), lambda i, t: (0, 0))],
        out_specs=tok,
        out_shape=jax.ShapeDtypeStruct((b, l, d), F32),
        name="residual",
        compiler_params=_params(("arbitrary", "arbitrary")),
    )(x1, moe, mod, final_w.reshape(1, d))


def _ba_rows(proj):
    b, l, _ = proj.shape
    ba = proj[:, :, COL_BA:COL_BA + 4 * HEADS].reshape(b, l // GDN_CHUNK, GDN_CHUNK, 2, 2, HEADS)
    return jnp.transpose(ba, (0, 1, 3, 4, 5, 2))


def _channel_ffn(h2, aff, x1, mod, lw, final_w, final_norm):
    b, n, d = h2.shape
    cap = EC_CAPACITY * n // N_EXPERTS
    aff_t = jnp.swapaxes(aff[:, :, :N_EXPERTS], 1, 2)
    slot_t = _route(aff_t, cap)
    merge = max(1, min(b, MOE_SET_TOKENS // n))
    while b % merge:
        merge -= 1
    if merge > 1:
        offset = (jnp.arange(b, dtype=I32) % merge * cap)[:, None, None]
        slot_t = jnp.where(slot_t >= 0, slot_t + offset, -1)
        join = lambda a: jnp.swapaxes(a.reshape(b // merge, merge, N_EXPERTS, n), 1, 2).reshape(
            b // merge, N_EXPERTS, merge * n)
        slot_t = join(slot_t)
    sets = lambda a: a.reshape(b // merge, merge * n, a.shape[-1])
    moe = _moe(sets(h2), slot_t, jnp.swapaxes(slot_t, 1, 2), sets(aff[:, :, :N_EXPERTS]),
               lw['w_gate'], lw['w_up'], lw['w_down'], lw['layer'], cap * merge)
    return _residual(x1, moe, mod, final_w, final_norm, merge)


def kernel(x, c, ctx, c_ctx, w_ada, b_ada, norm1_w, norm2_w, w_in, qkv_conv_w, gdn_a_log, gdn_dt_bias, gdn_norm_w, cmlp_ln_w, cmlp_ln_b, cmlp_w_s, cmlp_b_s, pool_w, pool_scale, w_br_a, w_br_b, w_br_c, w_out, w_router, w_gate, w_up, w_down, final_norm_w):
    b, l, d = x.shape
    depth = w_ada.shape[0]
    assert l % GDN_CHUNK == 0 and ctx.shape[1] % GDN_CHUNK == 0 and l % GRID_W == 0

    cc = jnp.concatenate([c, c_ctx[None, :], jnp.zeros((31 - b, d), F32)], axis=0)
    mods = _modulation(cc, w_ada, b_ada)
    zero_states = jnp.zeros((2, b, HEADS, HEAD_DIM, HEAD_DIM), F32)
    w_gate_bf, w_up_bf, w_down_bf = w_gate.astype(BF16), w_up.astype(BF16), w_down.astype(BF16)

    for layer in range(depth):
        last = layer == depth - 1
        mod_x = mods[layer, :b][:, None, :]
        mod_c = jnp.broadcast_to(mods[layer, b][None, None, :], mod_x.shape)

        wi = w_in[layer]
        wp = jnp.concatenate([wi[:, 0:4096], wi[:, 5664:8736], wi[:, 4128:5664], wi[:, 4096:4128],
                              jnp.zeros((d, PROJ_COLS - COL_BA - 4 * HEADS), F32)], axis=1).astype(BF16)
        wr = jnp.concatenate([w_router[layer], jnp.zeros((d, LANES - N_EXPERTS), F32)], axis=1)
        lw = {
            'gdn_norm_w': gdn_norm_w[layer].reshape(1, HEAD_DIM),
            'cmlp_ln_w': cmlp_ln_w[layer].reshape(1, CMLP_WIDTH),
            'cmlp_ln_b': cmlp_ln_b[layer].reshape(1, CMLP_WIDTH),
            'cmlp_w_s': cmlp_w_s[layer].astype(BF16),
            'cmlp_b_s': jnp.broadcast_to(cmlp_b_s[layer][:, :, None],
                                         (CMLP_GROUPS, CMLP_CHUNK, CMLP_WIDTH // CMLP_GROUPS)),
            'pool_w': pool_w[layer].astype(BF16),
            'pool_scale': pool_scale[layer].reshape(1, POOL_WIDTH),
            'w_br_a': w_br_a[layer].astype(BF16), 'w_br_b': w_br_b[layer].astype(BF16),
            'w_br_c': w_br_c[layer].astype(BF16), 'w_out': w_out[layer].astype(BF16),
            'norm2_w': norm2_w[layer].reshape(1, d), 'w_router': wr,
            'w_gate': w_gate_bf, 'w_up': w_up_bf, 'w_down': w_down_bf, 'layer': layer,
        }
        alog_b = jnp.broadcast_to(gdn_a_log[layer][:, :, None], (2, HEADS, GDN_CHUNK))
        dtb_b = jnp.broadcast_to(gdn_dt_bias[layer][:, :, None], (2, HEADS, GDN_CHUNK))

        def gdn_path(tokens, mod, states, shared_mod):
            nb, nl, _ = tokens.shape
            if shared_mod:
                per = max(1, min(nb, INPROJ_SET_TOKENS // nl))
                while nb % per:
                    per -= 1
                proj = _in_projection(tokens.reshape(nb // per, per * nl, d), mod[:nb // per], norm1_w[layer], wp)
                proj = proj.reshape(nb, nl, PROJ_COLS)
            else:
                proj = _in_projection(tokens, mod, norm1_w[layer], wp)
            q, k, v = _qkv_conv(proj, qkv_conv_w[layer])
            o_f, o_b, s_fin = _gdn(q, k, v, _ba_rows(proj), alog_b, dtb_b, states)
            return proj, (o_f, o_b), s_fin

        proj_c, o_c, ctx_states = gdn_path(ctx, mod_c, zero_states, True)
        proj_x, o_x, _ = gdn_path(x, mod_x, ctx_states, False)
        x1, h2, aff = _mix(o_x, proj_x, x, mod_x, lw, GRID_W)
        x = _channel_ffn(h2, aff, x1, mod_x, lw, final_norm_w, last)
        if not last:
            c1, h2c, affc = _mix(o_c, proj_c, ctx, mod_c, lw, ctx.shape[1])
            ctx = _channel_ffn(h2c, affc, c1, mod_c, lw, final_norm_w, False)
    return x
```

```python
import functools

import numpy as np
import jax
import jax.numpy as jnp
from jax import lax
from jax.experimental import pallas as pl
from jax.experimental.pallas import tpu as pltpu

F32 = jnp.float32
BF16 = jnp.bfloat16
I32 = jnp.int32

NORM_EPS = 1e-6
GRID_W = 64
HEADS = 8
HEAD_DIM = 128
GDN_WIDTH = HEADS * HEAD_DIM
QKV_CONV = 5
CMLP_GROUPS = 4
CMLP_CHUNK = 128
CMLP_WIDTH = 512
POOL_WINDOWS = (2, 4, 8, 16)
POOL_WIDTH = 512
N_EXPERTS = 16
EC_CAPACITY = 2

LANES = 128
GDN_CHUNK = 128
GDN_HEADS_PER_STEP = 8
INV_PASSES = 1
VMEM_LIMIT = 56 * 1024 * 1024

COL_Q, COL_K, COL_V, COL_Z = 0, 1024, 2048, 3072
COL_GATES = 4096
COL_U, COL_VG, COL_P = 7168, 7680, 8192
COL_BA = 8704
PROJ_COLS = 8960
PROJ_TN = 1792
INPROJ_SET_TOKENS = 2048


def _params(sem, vmem=VMEM_LIMIT):
    return pltpu.CompilerParams(dimension_semantics=sem, vmem_limit_bytes=vmem)


def _dot(a, b):
    return jnp.dot(a.astype(BF16), b.astype(BF16), preferred_element_type=F32)


def _split(a):
    hi = a.astype(BF16)
    lo = (a - hi.astype(F32)).astype(BF16)
    return hi, lo


def _dot_hp(a, b, passes):
    if passes == 1:
        return _dot(a, b)
    a_hi, a_lo = _split(a)
    b_hi, b_lo = _split(b)
    mm = lambda x, y: jnp.dot(x, y, preferred_element_type=F32)
    return mm(a_hi, b_hi) + (mm(a_hi, b_lo) + mm(a_lo, b_hi))


def _split3(a):
    a1 = a.astype(BF16)
    r1 = a - a1.astype(F32)
    a2 = r1.astype(BF16)
    a3 = (r1 - a2.astype(F32)).astype(BF16)
    return a1, a2, a3


def _dot_exact_rhs(a, b_bf16):
    a1, a2, a3 = _split3(a)
    mm = lambda x: jnp.dot(x, b_bf16, preferred_element_type=F32)
    return mm(a1) + (mm(a2) + mm(a3))


def _sigmoid(x):
    return 1.0 / (1.0 + jnp.exp(-x))


def _silu(x):
    return x * _sigmoid(x)


def _gelu_tanh(x):
    return 0.5 * x * (1.0 + jnp.tanh(0.7978845608028654 * (x + 0.044715 * (x * x * x))))


def _softplus(x):
    return jnp.maximum(x, 0.0) + jnp.log(1.0 + jnp.exp(-jnp.abs(x)))


def _rms(x, w):
    return x * lax.rsqrt(jnp.mean(x * x, axis=-1, keepdims=True) + NORM_EPS) * w


def _mod_kernel(c_ref, w_ref, b_ref, o_ref):
    s = _silu(c_ref[...])
    o_ref[0] = _dot_hp(s, w_ref[0], 3) + b_ref[0]


def _modulation(cc, w_ada, b_ada):
    depth, d, n6 = w_ada.shape
    rows = cc.shape[0]
    tn = 1536
    return pl.pallas_call(
        _mod_kernel,
        grid=(depth, n6 // tn),
        in_specs=[pl.BlockSpec((rows, d), lambda l, j: (0, 0)),
                  pl.BlockSpec((1, d, tn), lambda l, j: (l, 0, j)),
                  pl.BlockSpec((1, 1, tn), lambda l, j: (l, 0, j))],
        out_specs=pl.BlockSpec((1, rows, tn), lambda l, j: (l, 0, j)),
        out_shape=jax.ShapeDtypeStruct((depth, rows, n6), F32),
        name="adaln_mod",
        compiler_params=_params(("arbitrary", "arbitrary")),
    )(cc, w_ada, b_ada.reshape(depth, 1, n6))


def _inproj_kernel(x_ref, sh_ref, sc_ref, nw_ref, w_ref, o_ref, h_s):
    t = pl.program_id(2)
    tm = o_ref.shape[1]

    @pl.when(pl.program_id(1) == 0)
    def _():
        rows = pl.ds(pl.multiple_of(t * tm, tm), tm)
        h = _rms(x_ref[0, rows, :], nw_ref[...]) * (1.0 + sc_ref[0]) + sh_ref[0]
        h_s[t] = h.astype(BF16)

    o_ref[0] = jnp.dot(h_s[t], w_ref[...], preferred_element_type=F32)


def _in_projection(x, mod, norm_w, wp):
    b, l, d = x.shape
    tm = min(l, 1024)
    return pl.pallas_call(
        _inproj_kernel,
        grid=(b, PROJ_COLS // PROJ_TN, l // tm),
        in_specs=[pl.BlockSpec((1, l, d), lambda i, j, t: (i, 0, 0)),
                  pl.BlockSpec((1, 1, d), lambda i, j, t: (i, 0, 0)),
                  pl.BlockSpec((1, 1, d), lambda i, j, t: (i, 0, 1)),
                  pl.BlockSpec((1, d), lambda i, j, t: (0, 0)),
                  pl.BlockSpec((d, PROJ_TN), lambda i, j, t: (0, j))],
        out_specs=pl.BlockSpec((1, tm, PROJ_TN), lambda i, j, t: (i, t, j)),
        out_shape=jax.ShapeDtypeStruct((b, l, PROJ_COLS), F32),
        scratch_shapes=[pltpu.VMEM((l // tm, tm, d), BF16)],
        name="in_proj",
        compiler_params=_params(("arbitrary", "arbitrary", "arbitrary")),
    )(x, mod, mod, norm_w.reshape(1, d), wp)


CONV_PAD = 8


def _conv_kernel(q_ref, k_ref, v_ref, wq_ref, wk_ref, wv_ref, qo_ref, ko_ref, vo_ref, pad_s):
    l = q_ref.shape[1]
    width = q_ref.shape[2]
    zeros = jnp.zeros((CONV_PAD, width), F32)
    pad_s[pl.ds(0, CONV_PAD), :] = zeros
    pad_s[pl.ds(CONV_PAD + l, CONV_PAD), :] = zeros
    half = QKV_CONV // 2

    def conv_silu(x_ref, w_ref):
        pad_s[pl.ds(CONV_PAD, l), :] = x_ref[0]
        acc = None
        for j in range(QKV_CONV):
            term = pad_s[pl.ds(CONV_PAD - half + j, l), :] * w_ref[pl.ds(j, 1), :]
            acc = term if acc is None else acc + term
        return _silu(acc)

    def l2n(y):
        return y * lax.rsqrt(jnp.sum(y * y, axis=-1, keepdims=True) + NORM_EPS)

    yq = conv_silu(q_ref, wq_ref)
    yk = conv_silu(k_ref, wk_ref)
    yv = conv_silu(v_ref, wv_ref)
    for h in range(width // HEAD_DIM):
        sl = slice(h * HEAD_DIM, (h + 1) * HEAD_DIM)
        qo_ref[0, h] = l2n(yq[:, sl]) * (HEAD_DIM ** -0.5)
        ko_ref[0, h] = l2n(yk[:, sl])
        vo_ref[0, h] = yv[:, sl]


def _qkv_conv(proj, conv_w):
    b, l, _ = proj.shape
    width = 2 * HEAD_DIM
    nq = GDN_WIDTH // width
    hm = jax.ShapeDtypeStruct((b, HEADS, l, HEAD_DIM), F32)
    ospec = pl.BlockSpec((1, width // HEAD_DIM, l, HEAD_DIM), lambda i, j: (i, j, 0, 0))
    return pl.pallas_call(
        _conv_kernel,
        grid=(b, nq),
        in_specs=[pl.BlockSpec((1, l, width), lambda i, j: (i, 0, j)),
                  pl.BlockSpec((1, l, width), lambda i, j: (i, 0, nq + j)),
                  pl.BlockSpec((1, l, width), lambda i, j: (i, 0, 2 * nq + j)),
                  pl.BlockSpec((QKV_CONV, width), lambda i, j: (0, j)),
                  pl.BlockSpec((QKV_CONV, width), lambda i, j: (0, nq + j)),
                  pl.BlockSpec((QKV_CONV, width), lambda i, j: (0, 2 * nq + j))],
        out_specs=[ospec, ospec, ospec],
        out_shape=[hm, hm, hm],
        scratch_shapes=[pltpu.VMEM((l + 2 * CONV_PAD, width), F32)],
        name="qkv_conv",
        compiler_params=_params(("arbitrary", "arbitrary")),
    )(proj, proj, proj, conv_w, conv_w, conv_w)


def _gdn_kernel(qf_ref, kf_ref, vf_ref, qb_ref, kb_ref, vb_ref, baf_ref, bab_ref, alog_ref, dtb_ref,
                s0_ref, of_ref, ob_ref, sfin_ref, state_s, beta_s, gc_s, *, heads_per_step, inv_passes):
    c = GDN_CHUNK
    grp = pl.program_id(1)
    n = pl.program_id(2)

    @pl.when(n == 0)
    def _():
        state_s[...] = s0_ref[:, 0]

    ii = lax.broadcasted_iota(I32, (c, c), 0)
    jj = lax.broadcasted_iota(I32, (c, c), 1)
    xor = ii ^ jj
    eye = jnp.where(ii == jj, 1.0, 0.0)
    levels = c.bit_length() - 1
    later = (ii > jj, ii < jj)
    not_earlier = (ii >= jj, ii <= jj)
    last_idx = (c - 1, 0)
    q_refs, k_refs, v_refs = (qf_ref, qb_ref), (kf_ref, kb_ref), (vf_ref, vb_ref)
    o_refs = (of_ref, ob_ref)

    for d, ba_ref in enumerate((baf_ref, bab_ref)):
        gdec = -jnp.exp(alog_ref[d]) * _softplus(ba_ref[0, 0, 1, 0] + dtb_ref[d])
        cum01 = jnp.where(later[d], 0.0, 1.0).astype(BF16)
        beta_s[d] = _sigmoid(ba_ref[0, 0, 0, 0])
        gc_s[d] = _dot_exact_rhs(gdec, cum01)

    chains = [(d, i) for d in (0, 1) for i in range(heads_per_step)]
    cs = range(len(chains))
    g_rows, bcol, gcol = [], [], []
    for d, i in chains:
        hrow = grp * heads_per_step + i
        b_rows = jnp.broadcast_to(beta_s[d, pl.ds(hrow, 1), :], (c, c))
        g_rows.append(jnp.broadcast_to(gc_s[d, pl.ds(hrow, 1), :], (c, c)))
        bcol.append(b_rows.T)
        gcol.append(g_rows[-1].T)
    q_of = lambda x: q_refs[chains[x][0]][0, chains[x][1]]
    k_of = lambda x: k_refs[chains[x][0]][0, chains[x][1]]
    v_of = lambda x: v_refs[chains[x][0]][0, chains[x][1]]
    dir_of = lambda x: chains[x][0]

    kb = [k_of(x) * bcol[x] for x in cs]
    gram = [lax.dot_general(jnp.concatenate([kb[x], q_of(x)], axis=0).astype(BF16),
                            k_of(x).astype(BF16), (((1,), (1,)), ((), ())),
                            preferred_element_type=F32) for x in cs]
    decay = [jnp.exp(jnp.where(not_earlier[dir_of(x)], gcol[x] - g_rows[x], -jnp.inf)) for x in cs]
    m = [jnp.where(later[dir_of(x)], gram[x][:c] * decay[x], 0.0) for x in cs]
    attn = [gram[x][c:] * decay[x] for x in cs]

    t_inv = [eye - jnp.where(xor == 1, m[x], 0.0) for x in cs]
    for lvl in range(1, levels):
        z = [_dot_hp(jnp.where((xor >> lvl) == 1, m[x], 0.0), t_inv[x], inv_passes) for x in cs]
        t_inv = [t_inv[x] - _dot_hp(t_inv[x], z[x], inv_passes) for x in cs]

    egc = [jnp.exp(gcol[x]) for x in cs]
    uw = [_dot(t_inv[x], jnp.concatenate([v_of(x) * bcol[x], kb[x] * egc[x]], axis=1)) for x in cs]
    s_prev = [state_s[d, i] for d, i in chains]
    ws = [_dot(jnp.concatenate([uw[x][:, HEAD_DIM:], q_of(x) * egc[x]], axis=0), s_prev[x]) for x in cs]
    v_new = [uw[x][:, :HEAD_DIM] - ws[x][:c] for x in cs]
    av = [_dot(attn[x], v_new[x]) for x in cs]
    for x, (d, i) in enumerate(chains):
        o_refs[d][0, i] = ws[x][c:] + av[x]

    g_last = [jnp.broadcast_to(g_rows[x][:, last_idx[dir_of(x)]:last_idx[dir_of(x)] + 1], (c, c)) for x in cs]
    kdec_t = [(k_of(x) * jnp.exp(g_last[x] - gcol[x])).T for x in cs]
    upd = [_dot(kdec_t[x], v_new[x]) for x in cs]
    for x, (d, i) in enumerate(chains):
        state_s[d, i] = s_prev[x] * jnp.exp(g_last[x]) + upd[x]

    @pl.when(n == pl.num_programs(2) - 1)
    def _():
        sfin_ref[:, 0] = state_s[...]


def _gdn(q, k, v, ba_t, alog_b, dtb_b, s0):
    b, h, l, dh = q.shape
    c = GDN_CHUNK
    nch = l // c
    hg = GDN_HEADS_PER_STEP
    fwd = pl.BlockSpec((1, hg, c, dh), lambda i, g, nn: (i, g, nn, 0))
    bwd = pl.BlockSpec((1, hg, c, dh), lambda i, g, nn: (i, g, nch - 1 - nn, 0))
    st_spec = pl.BlockSpec((2, 1, hg, dh, dh), lambda i, g, nn: (0, i, g, 0, 0))
    par_spec = pl.BlockSpec((2, h, c), lambda i, g, nn: (0, 0, 0))
    hm = jax.ShapeDtypeStruct((b, h, l, dh), F32)
    return pl.pallas_call(
        functools.partial(_gdn_kernel, heads_per_step=hg, inv_passes=INV_PASSES),
        grid=(b, h // hg, nch),
        in_specs=[fwd, fwd, fwd, bwd, bwd, bwd,
                  pl.BlockSpec((1, 1, 2, 1, h, c), lambda i, g, nn: (i, nn, 0, 0, 0, 0)),
                  pl.BlockSpec((1, 1, 2, 1, h, c), lambda i, g, nn: (i, nch - 1 - nn, 0, 1, 0, 0)),
                  par_spec, par_spec, st_spec],
        out_specs=[fwd, bwd, st_spec],
        out_shape=[hm, hm, jax.ShapeDtypeStruct((2, b, h, dh, dh), F32)],
        scratch_shapes=[pltpu.VMEM((2, hg, dh, dh), F32),
                        pltpu.VMEM((2, h, c), F32),
                        pltpu.VMEM((2, h, c), F32)],
        name="gdn",
        compiler_params=_params(("arbitrary", "arbitrary", "arbitrary")),
    )(q, k, v, q, k, v, ba_t, ba_t, alog_b, dtb_b, s0)


MIX_TOKENS = 256


def _mix_kernel(of_ref, ob_ref, z_ref, ga_ref, gb_ref, gc_ref, u_ref, vg_ref, p_ref, x_ref,
                g1_ref, sh2_ref, sc2_ref, gnw_ref, lnw_ref, lnb_ref, ws_ref, bs_ref,
                pm_ref, icnt_ref, pw_ref, psc_ref, wa_ref, wb_ref, wc_ref, wo_ref,
                n2w_ref, wr_ref, x1_ref, h2_ref, aff_ref):
    tm = x_ref.shape[1]

    z = z_ref[0]
    gnw = gnw_ref[...]
    ya = []
    for h in range(HEADS):
        oh = of_ref[0, h] + ob_ref[0, h]
        zh = z[:, h * HEAD_DIM:(h + 1) * HEAD_DIM]
        ya.append(_rms(oh, gnw) * _silu(zh))
    pa = _dot(jnp.concatenate(ya, axis=1), wa_ref[...])

    u = _gelu_tanh(u_ref[0])
    vg = _gelu_tanh(vg_ref[0])
    mu = jnp.mean(vg, axis=-1, keepdims=True)
    var = jnp.mean(jnp.square(vg - mu), axis=-1, keepdims=True)
    vg = (vg - mu) * lax.rsqrt(var + NORM_EPS) * lnw_ref[...] + lnb_ref[...]
    gd = CMLP_WIDTH // CMLP_GROUPS
    rows = []
    for ch in range(tm // CMLP_CHUNK):
        r = slice(ch * CMLP_CHUNK, (ch + 1) * CMLP_CHUNK)
        cols = [_dot(ws_ref[g], vg[r, g * gd:(g + 1) * gd]) + bs_ref[g] for g in range(CMLP_GROUPS)]
        rows.append(jnp.concatenate(cols, axis=1))
    pb = _dot(u * jnp.concatenate(rows, axis=0), wb_ref[...])

    p = p_ref[0]
    psc = psc_ref[...]
    pg = POOL_WIDTH // len(POOL_WINDOWS)
    yc = []
    for g in range(len(POOL_WINDOWS)):
        pcol = p[:, g * pg:(g + 1) * pg]
        pooled = _dot_exact_rhs_lhs(pm_ref[g], pcol) * icnt_ref[g] - pcol
        yc.append(_dot(pooled, pw_ref[g]) * psc[:, g * pg:(g + 1) * pg])
    pc = _dot(jnp.concatenate(yc, axis=1), wc_ref[...])

    merged = _sigmoid(ga_ref[0]) * pa + _sigmoid(gb_ref[0]) * pb + _sigmoid(gc_ref[0]) * pc
    x1 = x_ref[0] + g1_ref[0] * _dot(merged, wo_ref[...])
    x1_ref[0] = x1

    h2 = _rms(x1, n2w_ref[...]) * (1.0 + sc2_ref[0]) + sh2_ref[0]
    h2_ref[0] = h2.astype(BF16)
    logits = _dot_hp(h2, wr_ref[...], 3)
    lane = lax.broadcasted_iota(I32, logits.shape, 1)
    logits = jnp.where(lane < N_EXPERTS, logits, -jnp.inf)
    e = jnp.exp(logits - jnp.max(logits, axis=-1, keepdims=True))
    aff_ref[0] = e / jnp.sum(e, axis=-1, keepdims=True)


def _dot_exact_rhs_lhs(a_bf16, b):
    b1, b2, b3 = _split3(b)
    mm = lambda y: jnp.dot(a_bf16, y, preferred_element_type=F32)
    return mm(b1) + (mm(b2) + mm(b3))


def _pool_constants(tm, seg):
    t = np.arange(tm)
    pos = t % seg
    base = t - pos
    mats, inv = [], []
    for w in POOL_WINDOWS:
        lo = np.clip(pos - w // 2, 0, seg) + base
        hi = np.clip(pos + w // 2, 0, seg) + base
        s = np.arange(tm)[None, :]
        mats.append(((s >= lo[:, None]) & (s < hi[:, None])).astype(np.float32))
        inv.append(np.broadcast_to((1.0 / (hi - lo).astype(np.float32))[:, None], (tm, POOL_WIDTH // len(POOL_WINDOWS))))
    return jnp.asarray(np.stack(mats), BF16), jnp.asarray(np.stack(inv), F32)


def _mix(o, proj, x, mod, lw, seg):
    b, l, d = x.shape
    tm = min(l, MIX_TOKENS)
    pm, icnt = _pool_constants(tm, seg)
    tok = lambda width, blk: pl.BlockSpec((1, tm, width), lambda i, t: (i, t, blk))
    modv = lambda blk: pl.BlockSpec((1, 1, d), lambda i, t: (i, 0, blk))
    full = lambda a: pl.BlockSpec(a.shape, lambda i, t: (0,) * a.ndim)
    consts = [lw['gdn_norm_w'], lw['cmlp_ln_w'], lw['cmlp_ln_b'], lw['cmlp_w_s'], lw['cmlp_b_s'],
              pm, icnt, lw['pool_w'], lw['pool_scale'], lw['w_br_a'], lw['w_br_b'], lw['w_br_c'],
              lw['w_out'], lw['norm2_w'], lw['w_router']]
    return pl.pallas_call(
        _mix_kernel,
        grid=(b, l // tm),
        in_specs=[pl.BlockSpec((1, HEADS, tm, HEAD_DIM), lambda i, t: (i, 0, t, 0)),
                  pl.BlockSpec((1, HEADS, tm, HEAD_DIM), lambda i, t: (i, 0, t, 0)),
                  tok(GDN_WIDTH, COL_Z // GDN_WIDTH),
                  tok(d, COL_GATES // d), tok(d, COL_GATES // d + 1), tok(d, COL_GATES // d + 2),
                  tok(CMLP_WIDTH, COL_U // CMLP_WIDTH), tok(CMLP_WIDTH, COL_VG // CMLP_WIDTH),
                  tok(POOL_WIDTH, COL_P // POOL_WIDTH),
                  tok(d, 0),
                  modv(2), modv(3), modv(4)] + [full(a) for a in consts],
        out_specs=[pl.BlockSpec((1, tm, d), lambda i, t: (i, t, 0)),
                   pl.BlockSpec((1, tm, d), lambda i, t: (i, t, 0)),
                   pl.BlockSpec((1, tm, LANES), lambda i, t: (i, t, 0))],
        out_shape=[jax.ShapeDtypeStruct((b, l, d), F32),
                   jax.ShapeDtypeStruct((b, l, d), BF16),
                   jax.ShapeDtypeStruct((b, l, LANES), F32)],
        name="mix",
        compiler_params=_params(("arbitrary", "arbitrary")),
    )(o[0], o[1], proj, proj, proj, proj, proj, proj, proj, x, mod, mod, mod, *consts)


F32_INF_BITS = 0x7F800000


ROUTE_ROWS = 256


def _route_kernel(a_ref, u_ref, slot_ref, *, cap):
    bits = pltpu.bitcast(a_ref[...], I32)
    capf = float(cap)

    def body(_, carry):
        lo, hi = carry
        mid = lo + ((hi - lo + 1) >> 1)
        cnt = jnp.sum(jnp.where(bits >= mid, 1.0, 0.0), axis=1, keepdims=True)
        ge = cnt >= capf
        return jnp.where(ge, mid, lo), jnp.where(ge, hi, mid - 1)

    rows = bits.shape[0]
    lo, _ = lax.fori_loop(0, 31, body, (jnp.zeros((rows, 1), I32),
                                        jnp.full((rows, 1), F32_INF_BITS, I32)))
    gt = bits > lo
    eq = bits == lo
    need = capf - jnp.sum(jnp.where(gt, 1.0, 0.0), axis=1, keepdims=True)
    upper = u_ref[...]
    eq_before = jnp.dot(jnp.where(eq, 1.0, 0.0).astype(BF16), upper, preferred_element_type=F32)
    sel = gt | (eq & (eq_before < need))
    pos = jnp.dot(jnp.where(sel, 1.0, 0.0).astype(BF16), upper, preferred_element_type=F32)
    slot_ref[...] = jnp.where(sel, pos.astype(I32), -1)


def _route(aff_t, cap):
    b, e, n = aff_t.shape
    rows = min(b * e, ROUTE_ROWS)
    upper = jnp.asarray(np.triu(np.ones((n, n), np.float32), 1), BF16)
    slots = pl.pallas_call(
        functools.partial(_route_kernel, cap=cap),
        grid=(b * e // rows,),
        in_specs=[pl.BlockSpec((rows, n), lambda i: (i, 0)),
                  pl.BlockSpec((n, n), lambda i: (0, 0))],
        out_specs=pl.BlockSpec((rows, n), lambda i: (i, 0)),
        out_shape=jax.ShapeDtypeStruct((b * e, n), I32),
        name="route",
        compiler_params=_params(("arbitrary",)),
    )(aff_t.reshape(b * e, n), upper)
    return slots.reshape(b, e, n)


MOE_SCATTER_ROWS = 512
MOE_SET_TOKENS = 2048


MOE_EXPERTS_PER_STEP = 2


def _moe_kernel(h_ref, slot_t_ref, slot_c_ref, aff_ref, wg_ref, wu_ref, wd_ref, o_ref, *, cap):
    step = pl.program_id(1)
    n = h_ref.shape[1]
    es = range(MOE_EXPERTS_PER_STEP)
    experts = [step * MOE_EXPERTS_PER_STEP + i for i in es]

    @pl.when(step == 0)
    def _():
        o_ref[...] = jnp.zeros_like(o_ref)

    slot_rows = lax.broadcasted_iota(I32, (cap, n), 0)
    sel = [jnp.where(slot_t_ref[0, pl.ds(e, 1), :] == slot_rows, 1.0, 0.0).astype(BF16) for e in experts]
    xe = [jnp.dot(sel[i], h_ref[0], preferred_element_type=F32).astype(BF16) for i in es]
    gate = [jnp.dot(xe[i], wg_ref[0, i], preferred_element_type=F32) for i in es]
    up = [jnp.dot(xe[i], wu_ref[0, i], preferred_element_type=F32) for i in es]
    ye = [_dot(_silu(gate[i]) * up[i], wd_ref[0, i]).astype(BF16) for i in es]

    tr = min(n, MOE_SCATTER_ROWS)
    lane = lax.broadcasted_iota(I32, (tr, N_EXPERTS), 1)
    slot_iota = lax.broadcasted_iota(I32, (tr, cap), 1).astype(F32)
    for r in range(n // tr):
        rs = pl.ds(r * tr, tr)
        total = None
        for i, e in enumerate(experts):
            scol = jnp.sum(jnp.where(lane == e, slot_c_ref[0, rs, :].astype(F32), 0.0), axis=1, keepdims=True)
            acol = jnp.sum(jnp.where(lane == e, aff_ref[0, rs, :], 0.0), axis=1, keepdims=True)
            put = jnp.where(scol == slot_iota, 1.0, 0.0).astype(BF16)
            part = acol * jnp.dot(put, ye[i], preferred_element_type=F32)
            total = part if total is None else total + part
        o_ref[0, rs, :] += total


def _moe(h2, slot_t, slot_c, aff, wg, wu, wd, layer, cap):
    g, n, d = h2.shape
    ff = wg.shape[3]
    per = MOE_EXPERTS_PER_STEP
    once = pl.Buffered(1)
    return pl.pallas_call(
        functools.partial(_moe_kernel, cap=cap),
        grid=(g, N_EXPERTS // per),
        in_specs=[pl.BlockSpec((1, n, d), lambda i, e: (i, 0, 0), pipeline_mode=once),
                  pl.BlockSpec((1, N_EXPERTS, n), lambda i, e: (i, 0, 0), pipeline_mode=once),
                  pl.BlockSpec((1, n, N_EXPERTS), lambda i, e: (i, 0, 0), pipeline_mode=once),
                  pl.BlockSpec((1, n, N_EXPERTS), lambda i, e: (i, 0, 0), pipeline_mode=once),
                  pl.BlockSpec((1, per, d, ff), lambda i, e: (layer, e, 0, 0)),
                  pl.BlockSpec((1, per, d, ff), lambda i, e: (layer, e, 0, 0)),
                  pl.BlockSpec((1, per, ff, d), lambda i, e: (layer, e, 0, 0))],
        out_specs=pl.BlockSpec((1, n, d), lambda i, e: (i, 0, 0)),
        out_shape=jax.ShapeDtypeStruct((g, n, d), F32),
        name="moe",
        compiler_params=_params(("arbitrary", "arbitrary")),
    )(h2, slot_t, slot_c, aff, wg, wu, wd)


def _resid_kernel(x_ref, y_ref, g_ref, w_ref, o_ref, *, final_norm):
    x = x_ref[0] + g_ref[0] * y_ref[0]
    o_ref[0] = _rms(x, w_ref[...]) if final_norm else x


def _residual(x1, moe, mod, final_w, final_norm, merge):
    b, l, d = x1.shape
    tm = min(l, 512)
    tok = pl.BlockSpec((1, tm, d), lambda i, t: (i, t, 0))
    return pl.pallas_call(
        functools.partial(_resid_kernel, final_norm=final_norm),
        grid=(b, l // tm),
        in_specs=[tok,
                  pl.BlockSpec((1, tm, d), lambda i, t: (i // merge, (i % merge) * (l // tm) + t, 0)),
                  pl.BlockSpec((1, 1, d), lambda i, t: (i, 0, 5)),
                  pl.BlockSpec((1, d), lambda i, t: (0, 0))],
        out_specs=tok,
        out_shape=jax.ShapeDtypeStruct((b, l, d), F32),
        name="residual",
        compiler_params=_params(("arbitrary", "arbitrary")),
    )(x1, moe, mod, final_w.reshape(1, d))


def _ba_rows(proj):
    b, l, _ = proj.shape
    ba = proj[:, :, COL_BA:COL_BA + 4 * HEADS].reshape(b, l // GDN_CHUNK, GDN_CHUNK, 2, 2, HEADS)
    return jnp.transpose(ba, (0, 1, 3, 4, 5, 2))


def _channel_ffn(h2, aff, x1, mod, lw, final_w, final_norm):
    b, n, d = h2.shape
    cap = EC_CAPACITY * n // N_EXPERTS
    aff_t = jnp.swapaxes(aff[:, :, :N_EXPERTS], 1, 2)
    slot_t = _route(aff_t, cap)
    merge = max(1, min(b, MOE_SET_TOKENS // n))
    while b % merge:
        merge -= 1
    if merge > 1:
        offset = (jnp.arange(b, dtype=I32) % merge * cap)[:, None, None]
        slot_t = jnp.where(slot_t >= 0, slot_t + offset, -1)
        join = lambda a: jnp.swapaxes(a.reshape(b // merge, merge, N_EXPERTS, n), 1, 2).reshape(
            b // merge, N_EXPERTS, merge * n)
        slot_t = join(slot_t)
    sets = lambda a: a.reshape(b // merge, merge * n, a.shape[-1])
    moe = _moe(sets(h2), slot_t, jnp.swapaxes(slot_t, 1, 2), sets(aff[:, :, :N_EXPERTS]),
               lw['w_gate'], lw['w_up'], lw['w_down'], lw['layer'], cap * merge)
    return _residual(x1, moe, mod, final_w, final_norm, merge)


def kernel(x, c, ctx, c_ctx, w_ada, b_ada, norm1_w, norm2_w, w_in, qkv_conv_w, gdn_a_log, gdn_dt_bias, gdn_norm_w, cmlp_ln_w, cmlp_ln_b, cmlp_w_s, cmlp_b_s, pool_w, pool_scale, w_br_a, w_br_b, w_br_c, w_out, w_router, w_gate, w_up, w_down, final_norm_w):
    b, l, d = x.shape
    depth = w_ada.shape[0]
    assert l % GDN_CHUNK == 0 and ctx.shape[1] % GDN_CHUNK == 0 and l % GRID_W == 0

    cc = jnp.concatenate([c, c_ctx[None, :], jnp.zeros((31 - b, d), F32)], axis=0)
    mods = _modulation(cc, w_ada, b_ada)
    zero_states = jnp.zeros((2, b, HEADS, HEAD_DIM, HEAD_DIM), F32)
    w_gate_bf, w_up_bf, w_down_bf = w_gate.astype(BF16), w_up.astype(BF16), w_down.astype(BF16)

    for layer in range(depth):
        last = layer == depth - 1
        mod_x = mods[layer, :b][:, None, :]
        mod_c = jnp.broadcast_to(mods[layer, b][None, None, :], mod_x.shape)

        wi = w_in[layer]
        wp = jnp.concatenate([wi[:, 0:4096], wi[:, 5664:8736], wi[:, 4128:5664], wi[:, 4096:4128],
                              jnp.zeros((d, PROJ_COLS - COL_BA - 4 * HEADS), F32)], axis=1).astype(BF16)
        wr = jnp.concatenate([w_router[layer], jnp.zeros((d, LANES - N_EXPERTS), F32)], axis=1)
        lw = {
            'gdn_norm_w': gdn_norm_w[layer].reshape(1, HEAD_DIM),
            'cmlp_ln_w': cmlp_ln_w[layer].reshape(1, CMLP_WIDTH),
            'cmlp_ln_b': cmlp_ln_b[layer].reshape(1, CMLP_WIDTH),
            'cmlp_w_s': cmlp_w_s[layer].astype(BF16),
            'cmlp_b_s': jnp.broadcast_to(cmlp_b_s[layer][:, :, None],
                                         (CMLP_GROUPS, CMLP_CHUNK, CMLP_WIDTH // CMLP_GROUPS)),
            'pool_w': pool_w[layer].astype(BF16),
            'pool_scale': pool_scale[layer].reshape(1, POOL_WIDTH),
            'w_br_a': w_br_a[layer].astype(BF16), 'w_br_b': w_br_b[layer].astype(BF16),
            'w_br_c': w_br_c[layer].astype(BF16), 'w_out': w_out[layer].astype(BF16),
            'norm2_w': norm2_w[layer].reshape(1, d), 'w_router': wr,
            'w_gate': w_gate_bf, 'w_up': w_up_bf, 'w_down': w_down_bf, 'layer': layer,
        }
        alog_b = jnp.broadcast_to(gdn_a_log[layer][:, :, None], (2, HEADS, GDN_CHUNK))
        dtb_b = jnp.broadcast_to(gdn_dt_bias[layer][:, :, None], (2, HEADS, GDN_CHUNK))

        def gdn_path(tokens, mod, states, shared_mod):
            nb, nl, _ = tokens.shape
            if shared_mod:
                per = max(1, min(nb, INPROJ_SET_TOKENS // nl))
                while nb % per:
                    per -= 1
                proj = _in_projection(tokens.reshape(nb // per, per * nl, d), mod[:nb // per], norm1_w[layer], wp)
                proj = proj.reshape(nb, nl, PROJ_COLS)
            else:
                proj = _in_projection(tokens, mod, norm1_w[layer], wp)
            q, k, v = _qkv_conv(proj, qkv_conv_w[layer])
            o_f, o_b, s_fin = _gdn(q, k, v, _ba_rows(proj), alog_b, dtb_b, states)
            return proj, (o_f, o_b), s_fin

        proj_c, o_c, ctx_states = gdn_path(ctx, mod_c, zero_states, True)
        proj_x, o_x, _ = gdn_path(x, mod_x, ctx_states, False)
        x1, h2, aff = _mix(o_x, proj_x, x, mod_x, lw, GRID_W)
        x = _channel_ffn(h2, aff, x1, mod_x, lw, final_norm_w, last)
        if not last:
            c1, h2c, affc = _mix(o_c, proj_c, ctx, mod_c, lw, ctx.shape[1])
            ctx = _channel_ffn(h2c, affc, c1, mod_c, lw, final_norm_w, False)
    return x
```

```python
import functools

import numpy as np
import jax
import jax.numpy as jnp
from jax import lax
from jax.experimental import pallas as pl
from jax.experimental.pallas import tpu as pltpu

F32 = jnp.float32
BF16 = jnp.bfloat16
I32 = jnp.int32

NORM_EPS = 1e-6
GRID_W = 64
HEADS = 8
HEAD_DIM = 128
GDN_WIDTH = HEADS * HEAD_DIM
QKV_CONV = 5
CMLP_GROUPS = 4
CMLP_CHUNK = 128
CMLP_WIDTH = 512
POOL_WINDOWS = (2, 4, 8, 16)
POOL_WIDTH = 512
N_EXPERTS = 16
EC_CAPACITY = 2

LANES = 128
GDN_CHUNK = 128
GDN_HEADS_PER_STEP = 8
INV_PASSES = 1
VMEM_LIMIT = 56 * 1024 * 1024

COL_Q, COL_K, COL_V, COL_Z = 0, 1024, 2048, 3072
COL_GATES = 4096
COL_U, COL_VG, COL_P = 7168, 7680, 8192
COL_BA = 8704
PROJ_COLS = 8960
PROJ_TN = 1792
INPROJ_SET_TOKENS = 2048


def _params(sem, vmem=VMEM_LIMIT):
    return pltpu.CompilerParams(dimension_semantics=sem, vmem_limit_bytes=vmem)


def _dot(a, b):
    return jnp.dot(a.astype(BF16), b.astype(BF16), preferred_element_type=F32)


def _split(a):
    hi = a.astype(BF16)
    lo = (a - hi.astype(F32)).astype(BF16)
    return hi, lo


def _dot_hp(a, b, passes):
    if passes == 1:
        return _dot(a, b)
    a_hi, a_lo = _split(a)
    b_hi, b_lo = _split(b)
    mm = lambda x, y: jnp.dot(x, y, preferred_element_type=F32)
    return mm(a_hi, b_hi) + (mm(a_hi, b_lo) + mm(a_lo, b_hi))


def _split3(a):
    a1 = a.astype(BF16)
    r1 = a - a1.astype(F32)
    a2 = r1.astype(BF16)
    a3 = (r1 - a2.astype(F32)).astype(BF16)
    return a1, a2, a3


def _dot_exact_rhs(a, b_bf16):
    a1, a2, a3 = _split3(a)
    mm = lambda x: jnp.dot(x, b_bf16, preferred_element_type=F32)
    return mm(a1) + (mm(a2) + mm(a3))


def _sigmoid(x):
    return 1.0 / (1.0 + jnp.exp(-x))


def _silu(x):
    return x * _sigmoid(x)


def _gelu_tanh(x):
    return 0.5 * x * (1.0 + jnp.tanh(0.7978845608028654 * (x + 0.044715 * (x * x * x))))


def _softplus(x):
    return jnp.maximum(x, 0.0) + jnp.log(1.0 + jnp.exp(-jnp.abs(x)))


def _rms(x, w):
    return x * lax.rsqrt(jnp.mean(x * x, axis=-1, keepdims=True) + NORM_EPS) * w


def _mod_kernel(c_ref, w_ref, b_ref, o_ref):
    s = _silu(c_ref[...])
    o_ref[0] = _dot_hp(s, w_ref[0], 3) + b_ref[0]


def _modulation(cc, w_ada, b_ada):
    depth, d, n6 = w_ada.shape
    rows = cc.shape[0]
    tn = 1536
    return pl.pallas_call(
        _mod_kernel,
        grid=(depth, n6 // tn),
        in_specs=[pl.BlockSpec((rows, d), lambda l, j: (0, 0)),
                  pl.BlockSpec((1, d, tn), lambda l, j: (l, 0, j)),
                  pl.BlockSpec((1, 1, tn), lambda l, j: (l, 0, j))],
        out_specs=pl.BlockSpec((1, rows, tn), lambda l, j: (l, 0, j)),
        out_shape=jax.ShapeDtypeStruct((depth, rows, n6), F32),
        name="adaln_mod",
        compiler_params=_params(("arbitrary", "arbitrary")),
    )(cc, w_ada, b_ada.reshape(depth, 1, n6))


def _inproj_kernel(x_ref, sh_ref, sc_ref, nw_ref, w_ref, o_ref, h_s):
    t = pl.program_id(2)
    tm = o_ref.shape[1]

    @pl.when(pl.program_id(1) == 0)
    def _():
        rows = pl.ds(pl.multiple_of(t * tm, tm), tm)
        h = _rms(x_ref[0, rows, :], nw_ref[...]) * (1.0 + sc_ref[0]) + sh_ref[0]
        h_s[t] = h.astype(BF16)

    o_ref[0] = jnp.dot(h_s[t], w_ref[...], preferred_element_type=F32)


def _in_projection(x, mod, norm_w, wp):
    b, l, d = x.shape
    tm = min(l, 1024)
    return pl.pallas_call(
        _inproj_kernel,
        grid=(b, PROJ_COLS // PROJ_TN, l // tm),
        in_specs=[pl.BlockSpec((1, l, d), lambda i, j, t: (i, 0, 0)),
                  pl.BlockSpec((1, 1, d), lambda i, j, t: (i, 0, 0)),
                  pl.BlockSpec((1, 1, d), lambda i, j, t: (i, 0, 1)),
                  pl.BlockSpec((1, d), lambda i, j, t: (0, 0)),
                  pl.BlockSpec((d, PROJ_TN), lambda i, j, t: (0, j))],
        out_specs=pl.BlockSpec((1, tm, PROJ_TN), lambda i, j, t: (i, t, j)),
        out_shape=jax.ShapeDtypeStruct((b, l, PROJ_COLS), F32),
        scratch_shapes=[pltpu.VMEM((l // tm, tm, d), BF16)],
        name="in_proj",
        compiler_params=_params(("arbitrary", "arbitrary", "arbitrary")),
    )(x, mod, mod, norm_w.reshape(1, d), wp)


CONV_PAD = 8


def _conv_kernel(q_ref, k_ref, v_ref, wq_ref, wk_ref, wv_ref, qo_ref, ko_ref, vo_ref, pad_s):
    l = q_ref.shape[1]
    width = q_ref.shape[2]
    zeros = jnp.zeros((CONV_PAD, width), F32)
    pad_s[pl.ds(0, CONV_PAD), :] = zeros
    pad_s[pl.ds(CONV_PAD + l, CONV_PAD), :] = zeros
    half = QKV_CONV // 2

    def conv_silu(x_ref, w_ref):
        pad_s[pl.ds(CONV_PAD, l), :] = x_ref[0]
        acc = None
        for j in range(QKV_CONV):
            term = pad_s[pl.ds(CONV_PAD - half + j, l), :] * w_ref[pl.ds(j, 1), :]
            acc = term if acc is None else acc + term
        return _silu(acc)

    def l2n(y):
        return y * lax.rsqrt(jnp.sum(y * y, axis=-1, keepdims=True) + NORM_EPS)

    yq = conv_silu(q_ref, wq_ref)
    yk = conv_silu(k_ref, wk_ref)
    yv = conv_silu(v_ref, wv_ref)
    for h in range(width // HEAD_DIM):
        sl = slice(h * HEAD_DIM, (h + 1) * HEAD_DIM)
        qo_ref[0, h] = l2n(yq[:, sl]) * (HEAD_DIM ** -0.5)
        ko_ref[0, h] = l2n(yk[:, sl])
        vo_ref[0, h] = yv[:, sl]


def _qkv_conv(proj, conv_w):
    b, l, _ = proj.shape
    width = 2 * HEAD_DIM
    nq = GDN_WIDTH // width
    hm = jax.ShapeDtypeStruct((b, HEADS, l, HEAD_DIM), F32)
    ospec = pl.BlockSpec((1, width // HEAD_DIM, l, HEAD_DIM), lambda i, j: (i, j, 0, 0))
    return pl.pallas_call(
        _conv_kernel,
        grid=(b, nq),
        in_specs=[pl.BlockSpec((1, l, width), lambda i, j: (i, 0, j)),
                  pl.BlockSpec((1, l, width), lambda i, j: (i, 0, nq + j)),
                  pl.BlockSpec((1, l, width), lambda i, j: (i, 0, 2 * nq + j)),
                  pl.BlockSpec((QKV_CONV, width), lambda i, j: (0, j)),
                  pl.BlockSpec((QKV_CONV, width), lambda i, j: (0, nq + j)),
                  pl.BlockSpec((QKV_CONV, width), lambda i, j: (0, 2 * nq + j))],
        out_specs=[ospec, ospec, ospec],
        out_shape=[hm, hm, hm],
        scratch_shapes=[pltpu.VMEM((l + 2 * CONV_PAD, width), F32)],
        name="qkv_conv",
        compiler_params=_params(("arbitrary", "arbitrary")),
    )(proj, proj, proj, conv_w, conv_w, conv_w)


def _gdn_kernel(qf_ref, kf_ref, vf_ref, qb_ref, kb_ref, vb_ref, baf_ref, bab_ref, alog_ref, dtb_ref,
                s0_ref, of_ref, ob_ref, sfin_ref, state_s, beta_s, gc_s, *, heads_per_step, inv_passes):
    c = GDN_CHUNK
    grp = pl.program_id(1)
    n = pl.program_id(2)

    @pl.when(n == 0)
    def _():
        state_s[...] = s0_ref[:, 0]

    ii = lax.broadcasted_iota(I32, (c, c), 0)
    jj = lax.broadcasted_iota(I32, (c, c), 1)
    xor = ii ^ jj
    eye = jnp.where(ii == jj, 1.0, 0.0)
    levels = c.bit_length() - 1
    later = (ii > jj, ii < jj)
    not_earlier = (ii >= jj, ii <= jj)
    last_idx = (c - 1, 0)
    q_refs, k_refs, v_refs = (qf_ref, qb_ref), (kf_ref, kb_ref), (vf_ref, vb_ref)
    o_refs = (of_ref, ob_ref)

    for d, ba_ref in enumerate((baf_ref, bab_ref)):
        gdec = -jnp.exp(alog_ref[d]) * _softplus(ba_ref[0, 0, 1, 0] + dtb_ref[d])
        cum01 = jnp.where(later[d], 0.0, 1.0).astype(BF16)
        beta_s[d] = _sigmoid(ba_ref[0, 0, 0, 0])
        gc_s[d] = _dot_exact_rhs(gdec, cum01)

    chains = [(d, i) for d in (0, 1) for i in range(heads_per_step)]
    cs = range(len(chains))
    g_rows, bcol, gcol = [], [], []
    for d, i in chains:
        hrow = grp * heads_per_step + i
        b_rows = jnp.broadcast_to(beta_s[d, pl.ds(hrow, 1), :], (c, c))
        g_rows.append(jnp.broadcast_to(gc_s[d, pl.ds(hrow, 1), :], (c, c)))
        bcol.append(b_rows.T)
        gcol.append(g_rows[-1].T)
    q_of = lambda x: q_refs[chains[x][0]][0, chains[x][1]]
    k_of = lambda x: k_refs[chains[x][0]][0, chains[x][1]]
    v_of = lambda x: v_refs[chains[x][0]][0, chains[x][1]]
    dir_of = lambda x: chains[x][0]

    kb = [k_of(x) * bcol[x] for x in cs]
    gram = [lax.dot_general(jnp.concatenate([kb[x], q_of(x)], axis=0).astype(BF16),
                            k_of(x).astype(BF16), (((1,), (1,)), ((), ())),
                            preferred_element_type=F32) for x in cs]
    decay = [jnp.exp(jnp.where(not_earlier[dir_of(x)], gcol[x] - g_rows[x], -jnp.inf)) for x in cs]
    m = [jnp.where(later[dir_of(x)], gram[x][:c] * decay[x], 0.0) for x in cs]
    attn = [gram[x][c:] * decay[x] for x in cs]

    t_inv = [eye - jnp.where(xor == 1, m[x], 0.0) for x in cs]
    for lvl in range(1, levels):
        z = [_dot_hp(jnp.where((xor >> lvl) == 1, m[x], 0.0), t_inv[x], inv_passes) for x in cs]
        t_inv = [t_inv[x] - _dot_hp(t_inv[x], z[x], inv_passes) for x in cs]

    egc = [jnp.exp(gcol[x]) for x in cs]
    uw = [_dot(t_inv[x], jnp.concatenate([v_of(x) * bcol[x], kb[x] * egc[x]], axis=1)) for x in cs]
    s_prev = [state_s[d, i] for d, i in chains]
    ws = [_dot(jnp.concatenate([uw[x][:, HEAD_DIM:], q_of(x) * egc[x]], axis=0), s_prev[x]) for x in cs]
    v_new = [uw[x][:, :HEAD_DIM] - ws[x][:c] for x in cs]
    av = [_dot(attn[x], v_new[x]) for x in cs]
    for x, (d, i) in enumerate(chains):
        o_refs[d][0, i] = ws[x][c:] + av[x]

    g_last = [jnp.broadcast_to(g_rows[x][:, last_idx[dir_of(x)]:last_idx[dir_of(x)] + 1], (c, c)) for x in cs]
    kdec_t = [(k_of(x) * jnp.exp(g_last[x] - gcol[x])).T for x in cs]
    upd = [_dot(kdec_t[x], v_new[x]) for x in cs]
    for x, (d, i) in enumerate(chains):
        state_s[d, i] = s_prev[x] * jnp.exp(g_last[x]) + upd[x]

    @pl.when(n == pl.num_programs(2) - 1)
    def _():
        sfin_ref[:, 0] = state_s[...]


def _gdn(q, k, v, ba_t, alog_b, dtb_b, s0):
    b, h, l, dh = q.shape
    c = GDN_CHUNK
    nch = l // c
    hg = GDN_HEADS_PER_STEP
    fwd = pl.BlockSpec((1, hg, c, dh), lambda i, g, nn: (i, g, nn, 0))
    bwd = pl.BlockSpec((1, hg, c, dh), lambda i, g, nn: (i, g, nch - 1 - nn, 0))
    st_spec = pl.BlockSpec((2, 1, hg, dh, dh), lambda i, g, nn: (0, i, g, 0, 0))
    par_spec = pl.BlockSpec((2, h, c), lambda i, g, nn: (0, 0, 0))
    hm = jax.ShapeDtypeStruct((b, h, l, dh), F32)
    return pl.pallas_call(
        functools.partial(_gdn_kernel, heads_per_step=hg, inv_passes=INV_PASSES),
        grid=(b, h // hg, nch),
        in_specs=[fwd, fwd, fwd, bwd, bwd, bwd,
                  pl.BlockSpec((1, 1, 2, 1, h, c), lambda i, g, nn: (i, nn, 0, 0, 0, 0)),
                  pl.BlockSpec((1, 1, 2, 1, h, c), lambda i, g, nn: (i, nch - 1 - nn, 0, 1, 0, 0)),
                  par_spec, par_spec, st_spec],
        out_specs=[fwd, bwd, st_spec],
        out_shape=[hm, hm, jax.ShapeDtypeStruct((2, b, h, dh, dh), F32)],
        scratch_shapes=[pltpu.VMEM((2, hg, dh, dh), F32),
                        pltpu.VMEM((2, h, c), F32),
                        pltpu.VMEM((2, h, c), F32)],
        name="gdn",
        compiler_params=_params(("arbitrary", "arbitrary", "arbitrary")),
    )(q, k, v, q, k, v, ba_t, ba_t, alog_b, dtb_b, s0)


MIX_TOKENS = 256


def _mix_kernel(of_ref, ob_ref, z_ref, ga_ref, gb_ref, gc_ref, u_ref, vg_ref, p_ref, x_ref,
                g1_ref, sh2_ref, sc2_ref, gnw_ref, lnw_ref, lnb_ref, ws_ref, bs_ref,
                pm_ref, icnt_ref, pw_ref, psc_ref, wa_ref, wb_ref, wc_ref, wo_ref,
                n2w_ref, wr_ref, x1_ref, h2_ref, aff_ref):
    tm = x_ref.shape[1]

    z = z_ref[0]
    gnw = gnw_ref[...]
    ya = []
    for h in range(HEADS):
        oh = of_ref[0, h] + ob_ref[0, h]
        zh = z[:, h * HEAD_DIM:(h + 1) * HEAD_DIM]
        ya.append(_rms(oh, gnw) * _silu(zh))
    pa = _dot(jnp.concatenate(ya, axis=1), wa_ref[...])

    u = _gelu_tanh(u_ref[0])
    vg = _gelu_tanh(vg_ref[0])
    mu = jnp.mean(vg, axis=-1, keepdims=True)
    var = jnp.mean(jnp.square(vg - mu), axis=-1, keepdims=True)
    vg = (vg - mu) * lax.rsqrt(var + NORM_EPS) * lnw_ref[...] + lnb_ref[...]
    gd = CMLP_WIDTH // CMLP_GROUPS
    rows = []
    for ch in range(tm // CMLP_CHUNK):
        r = slice(ch * CMLP_CHUNK, (ch + 1) * CMLP_CHUNK)
        cols = [_dot(ws_ref[g], vg[r, g * gd:(g + 1) * gd]) + bs_ref[g] for g in range(CMLP_GROUPS)]
        rows.append(jnp.concatenate(cols, axis=1))
    pb = _dot(u * jnp.concatenate(rows, axis=0), wb_ref[...])

    p = p_ref[0]
    psc = psc_ref[...]
    pg = POOL_WIDTH // len(POOL_WINDOWS)
    yc = []
    for g in range(len(POOL_WINDOWS)):
        pcol = p[:, g * pg:(g + 1) * pg]
        pooled = _dot_exact_rhs_lhs(pm_ref[g], pcol) * icnt_ref[g] - pcol
        yc.append(_dot(pooled, pw_ref[g]) * psc[:, g * pg:(g + 1) * pg])
    pc = _dot(jnp.concatenate(yc, axis=1), wc_ref[...])

    merged = _sigmoid(ga_ref[0]) * pa + _sigmoid(gb_ref[0]) * pb + _sigmoid(gc_ref[0]) * pc
    x1 = x_ref[0] + g1_ref[0] * _dot(merged, wo_ref[...])
    x1_ref[0] = x1

    h2 = _rms(x1, n2w_ref[...]) * (1.0 + sc2_ref[0]) + sh2_ref[0]
    h2_ref[0] = h2.astype(BF16)
    logits = _dot_hp(h2, wr_ref[...], 3)
    lane = lax.broadcasted_iota(I32, logits.shape, 1)
    logits = jnp.where(lane < N_EXPERTS, logits, -jnp.inf)
    e = jnp.exp(logits - jnp.max(logits, axis=-1, keepdims=True))
    aff_ref[0] = e / jnp.sum(e, axis=-1, keepdims=True)


def _dot_exact_rhs_lhs(a_bf16, b):
    b1, b2, b3 = _split3(b)
    mm = lambda y: jnp.dot(a_bf16, y, preferred_element_type=F32)
    return mm(b1) + (mm(b2) + mm(b3))


def _pool_constants(tm, seg):
    t = np.arange(tm)
    pos = t % seg
    base = t - pos
    mats, inv = [], []
    for w in POOL_WINDOWS:
        lo = np.clip(pos - w // 2, 0, seg) + base
        hi = np.clip(pos + w // 2, 0, seg) + base
        s = np.arange(tm)[None, :]
        mats.append(((s >= lo[:, None]) & (s < hi[:, None])).astype(np.float32))
        inv.append(np.broadcast_to((1.0 / (hi - lo).astype(np.float32))[:, None], (tm, POOL_WIDTH // len(POOL_WINDOWS))))
    return jnp.asarray(np.stack(mats), BF16), jnp.asarray(np.stack(inv), F32)


def _mix(o, proj, x, mod, lw, seg):
    b, l, d = x.shape
    tm = min(l, MIX_TOKENS)
    pm, icnt = _pool_constants(tm, seg)
    tok = lambda width, blk: pl.BlockSpec((1, tm, width), lambda i, t: (i, t, blk))
    modv = lambda blk: pl.BlockSpec((1, 1, d), lambda i, t: (i, 0, blk))
    full = lambda a: pl.BlockSpec(a.shape, lambda i, t: (0,) * a.ndim)
    consts = [lw['gdn_norm_w'], lw['cmlp_ln_w'], lw['cmlp_ln_b'], lw['cmlp_w_s'], lw['cmlp_b_s'],
              pm, icnt, lw['pool_w'], lw['pool_scale'], lw['w_br_a'], lw['w_br_b'], lw['w_br_c'],
              lw['w_out'], lw['norm2_w'], lw['w_router']]
    return pl.pallas_call(
        _mix_kernel,
        grid=(b, l // tm),
        in_specs=[pl.BlockSpec((1, HEADS, tm, HEAD_DIM), lambda i, t: (i, 0, t, 0)),
                  pl.BlockSpec((1, HEADS, tm, HEAD_DIM), lambda i, t: (i, 0, t, 0)),
                  tok(GDN_WIDTH, COL_Z // GDN_WIDTH),
                  tok(d, COL_GATES // d), tok(d, COL_GATES // d + 1), tok(d, COL_GATES // d + 2),
                  tok(CMLP_WIDTH, COL_U // CMLP_WIDTH), tok(CMLP_WIDTH, COL_VG // CMLP_WIDTH),
                  tok(POOL_WIDTH, COL_P // POOL_WIDTH),
                  tok(d, 0),
                  modv(2), modv(3), modv(4)] + [full(a) for a in consts],
        out_specs=[pl.BlockSpec((1, tm, d), lambda i, t: (i, t, 0)),
                   pl.BlockSpec((1, tm, d), lambda i, t: (i, t, 0)),
                   pl.BlockSpec((1, tm, LANES), lambda i, t: (i, t, 0))],
        out_shape=[jax.ShapeDtypeStruct((b, l, d), F32),
                   jax.ShapeDtypeStruct((b, l, d), BF16),
                   jax.ShapeDtypeStruct((b, l, LANES), F32)],
        name="mix",
        compiler_params=_params(("arbitrary", "arbitrary")),
    )(o[0], o[1], proj, proj, proj, proj, proj, proj, proj, x, mod, mod, mod, *consts)


F32_INF_BITS = 0x7F800000


ROUTE_ROWS = 256


def _route_kernel(a_ref, u_ref, slot_ref, *, cap):
    bits = pltpu.bitcast(a_ref[...], I32)
    capf = float(cap)

    def body(_, carry):
        lo, hi = carry
        mid = lo + ((hi - lo + 1) >> 1)
        cnt = jnp.sum(jnp.where(bits >= mid, 1.0, 0.0), axis=1, keepdims=True)
        ge = cnt >= capf
        return jnp.where(ge, mid, lo), jnp.where(ge, hi, mid - 1)

    rows = bits.shape[0]
    lo, _ = lax.fori_loop(0, 31, body, (jnp.zeros((rows, 1), I32),
                                        jnp.full((rows, 1), F32_INF_BITS, I32)))
    gt = bits > lo
    eq = bits == lo
    need = capf - jnp.sum(jnp.where(gt, 1.0, 0.0), axis=1, keepdims=True)
    upper = u_ref[...]
    eq_before = jnp.dot(jnp.where(eq, 1.0, 0.0).astype(BF16), upper, preferred_element_type=F32)
    sel = gt | (eq & (eq_before < need))
    pos = jnp.dot(jnp.where(sel, 1.0, 0.0).astype(BF16), upper, preferred_element_type=F32)
    slot_ref[...] = jnp.where(sel, pos.astype(I32), -1)


def _route(aff_t, cap):
    b, e, n = aff_t.shape
    rows = min(b * e, ROUTE_ROWS)
    upper = jnp.asarray(np.triu(np.ones((n, n), np.float32), 1), BF16)
    slots = pl.pallas_call(
        functools.partial(_route_kernel, cap=cap),
        grid=(b * e // rows,),
        in_specs=[pl.BlockSpec((rows, n), lambda i: (i, 0)),
                  pl.BlockSpec((n, n), lambda i: (0, 0))],
        out_specs=pl.BlockSpec((rows, n), lambda i: (i, 0)),
        out_shape=jax.ShapeDtypeStruct((b * e, n), I32),
        name="route",
        compiler_params=_params(("arbitrary",)),
    )(aff_t.reshape(b * e, n), upper)
    return slots.reshape(b, e, n)


MOE_SCATTER_ROWS = 512
MOE_SET_TOKENS = 2048


MOE_EXPERTS_PER_STEP = 1


def _moe_kernel(h_ref, slot_t_ref, slot_c_ref, aff_ref, wg_ref, wu_ref, wd_ref, o_ref, *, cap):
    step = pl.program_id(1)
    n = h_ref.shape[1]
    es = range(MOE_EXPERTS_PER_STEP)
    experts = [step * MOE_EXPERTS_PER_STEP + i for i in es]

    @pl.when(step == 0)
    def _():
        o_ref[...] = jnp.zeros_like(o_ref)

    slot_rows = lax.broadcasted_iota(I32, (cap, n), 0)
    sel = [jnp.where(slot_t_ref[0, pl.ds(e, 1), :] == slot_rows, 1.0, 0.0).astype(BF16) for e in experts]
    xe = [jnp.dot(sel[i], h_ref[0], preferred_element_type=F32).astype(BF16) for i in es]
    gate = [jnp.dot(xe[i], wg_ref[0, i], preferred_element_type=F32) for i in es]
    up = [jnp.dot(xe[i], wu_ref[0, i], preferred_element_type=F32) for i in es]
    ye = [_dot(_silu(gate[i]) * up[i], wd_ref[0, i]).astype(BF16) for i in es]

    tr = min(n, MOE_SCATTER_ROWS)
    lane = lax.broadcasted_iota(I32, (tr, N_EXPERTS), 1)
    slot_iota = lax.broadcasted_iota(I32, (tr, cap), 1).astype(F32)
    for r in range(n // tr):
        rs = pl.ds(r * tr, tr)
        total = None
        for i, e in enumerate(experts):
            scol = jnp.sum(jnp.where(lane == e, slot_c_ref[0, rs, :].astype(F32), 0.0), axis=1, keepdims=True)
            acol = jnp.sum(jnp.where(lane == e, aff_ref[0, rs, :], 0.0), axis=1, keepdims=True)
            put = jnp.where(scol == slot_iota, 1.0, 0.0).astype(BF16)
            part = acol * jnp.dot(put, ye[i], preferred_element_type=F32)
            total = part if total is None else total + part
        o_ref[0, rs, :] += total


def _moe(h2, slot_t, slot_c, aff, wg, wu, wd, layer, cap):
    g, n, d = h2.shape
    ff = wg.shape[3]
    per = MOE_EXPERTS_PER_STEP
    return pl.pallas_call(
        functools.partial(_moe_kernel, cap=cap),
        grid=(g, N_EXPERTS // per),
        in_specs=[pl.BlockSpec((1, n, d), lambda i, e: (i, 0, 0)),
                  pl.BlockSpec((1, N_EXPERTS, n), lambda i, e: (i, 0, 0)),
                  pl.BlockSpec((1, n, N_EXPERTS), lambda i, e: (i, 0, 0)),
                  pl.BlockSpec((1, n, N_EXPERTS), lambda i, e: (i, 0, 0)),
                  pl.BlockSpec((1, per, d, ff), lambda i, e: (layer, e, 0, 0)),
                  pl.BlockSpec((1, per, d, ff), lambda i, e: (layer, e, 0, 0)),
                  pl.BlockSpec((1, per, ff, d), lambda i, e: (layer, e, 0, 0))],
        out_specs=pl.BlockSpec((1, n, d), lambda i, e: (i, 0, 0)),
        out_shape=jax.ShapeDtypeStruct((g, n, d), F32),
        name="moe",
        compiler_params=_params(("arbitrary", "arbitrary")),
    )(h2, slot_t, slot_c, aff, wg, wu, wd)


def _resid_kernel(x_ref, y_ref, g_ref, w_ref, o_ref, *, final_norm):
    x = x_ref[0] + g_ref[0] * y_ref[0]
    o_ref[0] = _rms(x, w_ref[...]) if final_norm else x


def _residual(x1, moe, mod, final_w, final_norm, merge):
    b, l, d = x1.shape
    tm = min(l, 512)
    tok = pl.BlockSpec((1, tm, d), lambda i, t: (i, t, 0))
    return pl.pallas_call(
        functools.partial(_resid_kernel, final_norm=final_norm),
        grid=(b, l // tm),
        in_specs=[tok,
                  pl.BlockSpec((1, tm, d), lambda i, t: (i // merge, (i % merge) * (l // tm) + t, 0)),
                  pl.BlockSpec((1, 1, d), lambda i, t: (i, 0, 5)),
                  pl.BlockSpec((1, d), lambda i, t: (0, 0))],
        out_specs=tok,
        out_shape=jax.ShapeDtypeStruct((b, l, d), F32),
        name="residual",
        compiler_params=_params(("arbitrary", "arbitrary")),
    )(x1, moe, mod, final_w.reshape(1, d))


def _ba_rows(proj):
    b, l, _ = proj.shape
    ba = proj[:, :, COL_BA:COL_BA + 4 * HEADS].reshape(b, l // GDN_CHUNK, GDN_CHUNK, 2, 2, HEADS)
    return jnp.transpose(ba, (0, 1, 3, 4, 5, 2))


def _channel_ffn(h2, aff, x1, mod, lw, final_w, final_norm):
    b, n, d = h2.shape
    cap = EC_CAPACITY * n // N_EXPERTS
    aff_t = jnp.swapaxes(aff[:, :, :N_EXPERTS], 1, 2)
    slot_t = _route(aff_t, cap)
    merge = max(1, min(b, MOE_SET_TOKENS // n))
    while b % merge:
        merge -= 1
    if merge > 1:
        offset = (jnp.arange(b, dtype=I32) % merge * cap)[:, None, None]
        slot_t = jnp.where(slot_t >= 0, slot_t + offset, -1)
        join = lambda a: jnp.swapaxes(a.reshape(b // merge, merge, N_EXPERTS, n), 1, 2).reshape(
            b // merge, N_EXPERTS, merge * n)
        slot_t = join(slot_t)
    sets = lambda a: a.reshape(b // merge, merge * n, a.shape[-1])
    moe = _moe(sets(h2), slot_t, jnp.swapaxes(slot_t, 1, 2), sets(aff[:, :, :N_EXPERTS]),
               lw['w_gate'], lw['w_up'], lw['w_down'], lw['layer'], cap * merge)
    return _residual(x1, moe, mod, final_w, final_norm, merge)


def kernel(x, c, ctx, c_ctx, w_ada, b_ada, norm1_w, norm2_w, w_in, qkv_conv_w, gdn_a_log, gdn_dt_bias, gdn_norm_w, cmlp_ln_w, cmlp_ln_b, cmlp_w_s, cmlp_b_s, pool_w, pool_scale, w_br_a, w_br_b, w_br_c, w_out, w_router, w_gate, w_up, w_down, final_norm_w):
    b, l, d = x.shape
    depth = w_ada.shape[0]
    assert l % GDN_CHUNK == 0 and ctx.shape[1] % GDN_CHUNK == 0 and l % GRID_W == 0

    cc = jnp.concatenate([c, c_ctx[None, :], jnp.zeros((31 - b, d), F32)], axis=0)
    mods = _modulation(cc, w_ada, b_ada)
    zero_states = jnp.zeros((2, b, HEADS, HEAD_DIM, HEAD_DIM), F32)
    w_gate_bf, w_up_bf, w_down_bf = w_gate.astype(BF16), w_up.astype(BF16), w_down.astype(BF16)

    for layer in range(depth):
        last = layer == depth - 1
        mod_x = mods[layer, :b][:, None, :]
        mod_c = jnp.broadcast_to(mods[layer, b][None, None, :], mod_x.shape)

        wi = w_in[layer]
        wp = jnp.concatenate([wi[:, 0:4096], wi[:, 5664:8736], wi[:, 4128:5664], wi[:, 4096:4128],
                              jnp.zeros((d, PROJ_COLS - COL_BA - 4 * HEADS), F32)], axis=1).astype(BF16)
        wr = jnp.concatenate([w_router[layer], jnp.zeros((d, LANES - N_EXPERTS), F32)], axis=1)
        lw = {
            'gdn_norm_w': gdn_norm_w[layer].reshape(1, HEAD_DIM),
            'cmlp_ln_w': cmlp_ln_w[layer].reshape(1, CMLP_WIDTH),
            'cmlp_ln_b': cmlp_ln_b[layer].reshape(1, CMLP_WIDTH),
            'cmlp_w_s': cmlp_w_s[layer].astype(BF16),
            'cmlp_b_s': jnp.broadcast_to(cmlp_b_s[layer][:, :, None],
                                         (CMLP_GROUPS, CMLP_CHUNK, CMLP_WIDTH // CMLP_GROUPS)),
            'pool_w': pool_w[layer].astype(BF16),
            'pool_scale': pool_scale[layer].reshape(1, POOL_WIDTH),
            'w_br_a': w_br_a[layer].astype(BF16), 'w_br_b': w_br_b[layer].astype(BF16),
            'w_br_c': w_br_c[layer].astype(BF16), 'w_out': w_out[layer].astype(BF16),
            'norm2_w': norm2_w[layer].reshape(1, d), 'w_router': wr,
            'w_gate': w_gate_bf, 'w_up': w_up_bf, 'w_down': w_down_bf, 'layer': layer,
        }
        alog_b = jnp.broadcast_to(gdn_a_log[layer][:, :, None], (2, HEADS, GDN_CHUNK))
        dtb_b = jnp.broadcast_to(gdn_dt_bias[layer][:, :, None], (2, HEADS, GDN_CHUNK))

        def gdn_path(tokens, mod, states, shared_mod):
            nb, nl, _ = tokens.shape
            if shared_mod:
                per = max(1, min(nb, INPROJ_SET_TOKENS // nl))
                while nb % per:
                    per -= 1
                proj = _in_projection(tokens.reshape(nb // per, per * nl, d), mod[:nb // per], norm1_w[layer], wp)
                proj = proj.reshape(nb, nl, PROJ_COLS)
            else:
                proj = _in_projection(tokens, mod, norm1_w[layer], wp)
            q, k, v = _qkv_conv(proj, qkv_conv_w[layer])
            o_f, o_b, s_fin = _gdn(q, k, v, _ba_rows(proj), alog_b, dtb_b, states)
            return proj, (o_f, o_b), s_fin

        proj_c, o_c, ctx_states = gdn_path(ctx, mod_c, zero_states, True)
        proj_x, o_x, _ = gdn_path(x, mod_x, ctx_states, False)
        x1, h2, aff = _mix(o_x, proj_x, x, mod_x, lw, GRID_W)
        x = _channel_ffn(h2, aff, x1, mod_x, lw, final_norm_w, last)
        if not last:
            c1, h2c, affc = _mix(o_c, proj_c, ctx, mod_c, lw, ctx.shape[1])
            ctx = _channel_ffn(h2c, affc, c1, mod_c, lw, final_norm_w, False)
    return x
```

```python
import functools

import numpy as np
import jax
import jax.numpy as jnp
from jax import lax
from jax.experimental import pallas as pl
from jax.experimental.pallas import tpu as pltpu

F32 = jnp.float32
BF16 = jnp.bfloat16
I32 = jnp.int32

NORM_EPS = 1e-6
GRID_W = 64
HEADS = 8
HEAD_DIM = 128
GDN_WIDTH = HEADS * HEAD_DIM
QKV_CONV = 5
CMLP_GROUPS = 4
CMLP_CHUNK = 128
CMLP_WIDTH = 512
POOL_WINDOWS = (2, 4, 8, 16)
POOL_WIDTH = 512
N_EXPERTS = 16
EC_CAPACITY = 2

LANES = 128
GDN_CHUNK = 128
GDN_HEADS_PER_STEP = 8
INV_PASSES = 1
VMEM_LIMIT = 56 * 1024 * 1024

COL_Q, COL_K, COL_V, COL_Z = 0, 1024, 2048, 3072
COL_GATES = 4096
COL_U, COL_VG, COL_P = 7168, 7680, 8192
COL_BA = 8704
PROJ_COLS = 8960
PROJ_TN = 1792
INPROJ_SET_TOKENS = 2048


def _params(sem, vmem=VMEM_LIMIT):
    return pltpu.CompilerParams(dimension_semantics=sem, vmem_limit_bytes=vmem)


def _dot(a, b):
    return jnp.dot(a.astype(BF16), b.astype(BF16), preferred_element_type=F32)


def _split(a):
    hi = a.astype(BF16)
    lo = (a - hi.astype(F32)).astype(BF16)
    return hi, lo


def _dot_hp(a, b, passes):
    if passes == 1:
        return _dot(a, b)
    a_hi, a_lo = _split(a)
    b_hi, b_lo = _split(b)
    mm = lambda x, y: jnp.dot(x, y, preferred_element_type=F32)
    return mm(a_hi, b_hi) + (mm(a_hi, b_lo) + mm(a_lo, b_hi))


def _split3(a):
    a1 = a.astype(BF16)
    r1 = a - a1.astype(F32)
    a2 = r1.astype(BF16)
    a3 = (r1 - a2.astype(F32)).astype(BF16)
    return a1, a2, a3


def _dot_exact_rhs(a, b_bf16):
    a1, a2, a3 = _split3(a)
    mm = lambda x: jnp.dot(x, b_bf16, preferred_element_type=F32)
    return mm(a1) + (mm(a2) + mm(a3))


def _sigmoid(x):
    return 1.0 / (1.0 + jnp.exp(-x))


def _silu(x):
    return x * _sigmoid(x)


def _gelu_tanh(x):
    return 0.5 * x * (1.0 + jnp.tanh(0.7978845608028654 * (x + 0.044715 * (x * x * x))))


def _softplus(x):
    return jnp.maximum(x, 0.0) + jnp.log(1.0 + jnp.exp(-jnp.abs(x)))


def _rms(x, w):
    return x * lax.rsqrt(jnp.mean(x * x, axis=-1, keepdims=True) + NORM_EPS) * w


def _mod_kernel(c_ref, w_ref, b_ref, o_ref):
    s = _silu(c_ref[...])
    o_ref[0] = _dot_hp(s, w_ref[0], 3) + b_ref[0]


def _modulation(cc, w_ada, b_ada):
    depth, d, n6 = w_ada.shape
    rows = cc.shape[0]
    tn = 1536
    return pl.pallas_call(
        _mod_kernel,
        grid=(depth, n6 // tn),
        in_specs=[pl.BlockSpec((rows, d), lambda l, j: (0, 0)),
                  pl.BlockSpec((1, d, tn), lambda l, j: (l, 0, j)),
                  pl.BlockSpec((1, 1, tn), lambda l, j: (l, 0, j))],
        out_specs=pl.BlockSpec((1, rows, tn), lambda l, j: (l, 0, j)),
        out_shape=jax.ShapeDtypeStruct((depth, rows, n6), F32),
        name="adaln_mod",
        compiler_params=_params(("arbitrary", "arbitrary")),
    )(cc, w_ada, b_ada.reshape(depth, 1, n6))


def _inproj_kernel(x_ref, sh_ref, sc_ref, nw_ref, w_ref, o_ref, h_s):
    t = pl.program_id(2)
    tm = o_ref.shape[1]

    @pl.when(pl.program_id(1) == 0)
    def _():
        rows = pl.ds(pl.multiple_of(t * tm, tm), tm)
        h = _rms(x_ref[0, rows, :], nw_ref[...]) * (1.0 + sc_ref[0]) + sh_ref[0]
        h_s[t] = h.astype(BF16)

    o_ref[0] = jnp.dot(h_s[t], w_ref[...], preferred_element_type=F32)


STATE_TILES = (0, 1, 4)


def _state_tile(j):
    return j + (j // 2) * 2


def _in_projection(x, mod, norm_w, wp, states_only=False):
    b, l, d = x.shape
    tm = min(l, 1024)
    assert tuple(_state_tile(j) for j in range(len(STATE_TILES))) == STATE_TILES
    assert COL_V + GDN_WIDTH <= 2 * PROJ_TN and COL_BA // PROJ_TN == STATE_TILES[2]
    ncol = len(STATE_TILES) if states_only else PROJ_COLS // PROJ_TN
    wtile = _state_tile if states_only else (lambda j: j)
    return pl.pallas_call(
        _inproj_kernel,
        grid=(b, ncol, l // tm),
        in_specs=[pl.BlockSpec((1, l, d), lambda i, j, t: (i, 0, 0)),
                  pl.BlockSpec((1, 1, d), lambda i, j, t: (i, 0, 0)),
                  pl.BlockSpec((1, 1, d), lambda i, j, t: (i, 0, 1)),
                  pl.BlockSpec((1, d), lambda i, j, t: (0, 0)),
                  pl.BlockSpec((d, PROJ_TN), lambda i, j, t: (0, wtile(j)))],
        out_specs=pl.BlockSpec((1, tm, PROJ_TN), lambda i, j, t: (i, t, j)),
        out_shape=jax.ShapeDtypeStruct((b, l, ncol * PROJ_TN), F32),
        scratch_shapes=[pltpu.VMEM((l // tm, tm, d), BF16)],
        name="in_proj",
        compiler_params=_params(("arbitrary", "arbitrary", "arbitrary")),
    )(x, mod, mod, norm_w.reshape(1, d), wp)


CONV_PAD = 8


def _conv_kernel(q_ref, k_ref, v_ref, wq_ref, wk_ref, wv_ref, qo_ref, ko_ref, vo_ref, pad_s):
    l = q_ref.shape[1]
    width = q_ref.shape[2]
    zeros = jnp.zeros((CONV_PAD, width), F32)
    pad_s[pl.ds(0, CONV_PAD), :] = zeros
    pad_s[pl.ds(CONV_PAD + l, CONV_PAD), :] = zeros
    half = QKV_CONV // 2

    def conv_silu(x_ref, w_ref):
        pad_s[pl.ds(CONV_PAD, l), :] = x_ref[0]
        acc = None
        for j in range(QKV_CONV):
            term = pad_s[pl.ds(CONV_PAD - half + j, l), :] * w_ref[pl.ds(j, 1), :]
            acc = term if acc is None else acc + term
        return _silu(acc)

    def l2n(y):
        return y * lax.rsqrt(jnp.sum(y * y, axis=-1, keepdims=True) + NORM_EPS)

    yq = conv_silu(q_ref, wq_ref)
    yk = conv_silu(k_ref, wk_ref)
    yv = conv_silu(v_ref, wv_ref)
    for h in range(width // HEAD_DIM):
        sl = slice(h * HEAD_DIM, (h + 1) * HEAD_DIM)
        qo_ref[0, h] = l2n(yq[:, sl]) * (HEAD_DIM ** -0.5)
        ko_ref[0, h] = l2n(yk[:, sl])
        vo_ref[0, h] = yv[:, sl]


def _qkv_conv(proj, conv_w):
    b, l, _ = proj.shape
    width = 2 * HEAD_DIM
    nq = GDN_WIDTH // width
    hm = jax.ShapeDtypeStruct((b, HEADS, l, HEAD_DIM), F32)
    ospec = pl.BlockSpec((1, width // HEAD_DIM, l, HEAD_DIM), lambda i, j: (i, j, 0, 0))
    return pl.pallas_call(
        _conv_kernel,
        grid=(b, nq),
        in_specs=[pl.BlockSpec((1, l, width), lambda i, j: (i, 0, j)),
                  pl.BlockSpec((1, l, width), lambda i, j: (i, 0, nq + j)),
                  pl.BlockSpec((1, l, width), lambda i, j: (i, 0, 2 * nq + j)),
                  pl.BlockSpec((QKV_CONV, width), lambda i, j: (0, j)),
                  pl.BlockSpec((QKV_CONV, width), lambda i, j: (0, nq + j)),
                  pl.BlockSpec((QKV_CONV, width), lambda i, j: (0, 2 * nq + j))],
        out_specs=[ospec, ospec, ospec],
        out_shape=[hm, hm, hm],
        scratch_shapes=[pltpu.VMEM((l + 2 * CONV_PAD, width), F32)],
        name="qkv_conv",
        compiler_params=_params(("arbitrary", "arbitrary")),
    )(proj, proj, proj, conv_w, conv_w, conv_w)


def _gdn_kernel(qf_ref, kf_ref, vf_ref, qb_ref, kb_ref, vb_ref, baf_ref, bab_ref, alog_ref, dtb_ref,
                s0_ref, of_ref, ob_ref, sfin_ref, state_s, beta_s, gc_s, *, heads_per_step, inv_passes):
    c = GDN_CHUNK
    grp = pl.program_id(1)
    n = pl.program_id(2)

    @pl.when(n == 0)
    def _():
        state_s[...] = s0_ref[:, 0]

    ii = lax.broadcasted_iota(I32, (c, c), 0)
    jj = lax.broadcasted_iota(I32, (c, c), 1)
    xor = ii ^ jj
    eye = jnp.where(ii == jj, 1.0, 0.0)
    levels = c.bit_length() - 1
    later = (ii > jj, ii < jj)
    not_earlier = (ii >= jj, ii <= jj)
    last_idx = (c - 1, 0)
    q_refs, k_refs, v_refs = (qf_ref, qb_ref), (kf_ref, kb_ref), (vf_ref, vb_ref)
    o_refs = (of_ref, ob_ref)

    for d, ba_ref in enumerate((baf_ref, bab_ref)):
        gdec = -jnp.exp(alog_ref[d]) * _softplus(ba_ref[0, 0, 1, 0] + dtb_ref[d])
        cum01 = jnp.where(later[d], 0.0, 1.0).astype(BF16)
        beta_s[d] = _sigmoid(ba_ref[0, 0, 0, 0])
        gc_s[d] = _dot_exact_rhs(gdec, cum01)

    chains = [(d, i) for d in (0, 1) for i in range(heads_per_step)]
    cs = range(len(chains))
    g_rows, bcol, gcol = [], [], []
    for d, i in chains:
        hrow = grp * heads_per_step + i
        b_rows = jnp.broadcast_to(beta_s[d, pl.ds(hrow, 1), :], (c, c))
        g_rows.append(jnp.broadcast_to(gc_s[d, pl.ds(hrow, 1), :], (c, c)))
        bcol.append(b_rows.T)
        gcol.append(g_rows[-1].T)
    q_of = lambda x: q_refs[chains[x][0]][0, chains[x][1]]
    k_of = lambda x: k_refs[chains[x][0]][0, chains[x][1]]
    v_of = lambda x: v_refs[chains[x][0]][0, chains[x][1]]
    dir_of = lambda x: chains[x][0]

    kb = [k_of(x) * bcol[x] for x in cs]
    gram = [lax.dot_general(jnp.concatenate([kb[x], q_of(x)], axis=0).astype(BF16),
                            k_of(x).astype(BF16), (((1,), (1,)), ((), ())),
                            preferred_element_type=F32) for x in cs]
    decay = [jnp.exp(jnp.where(not_earlier[dir_of(x)], gcol[x] - g_rows[x], -jnp.inf)) for x in cs]
    m = [jnp.where(later[dir_of(x)], gram[x][:c] * decay[x], 0.0) for x in cs]
    attn = [gram[x][c:] * decay[x] for x in cs]

    t_inv = [eye - jnp.where(xor == 1, m[x], 0.0) for x in cs]
    for lvl in range(1, levels):
        z = [_dot_hp(jnp.where((xor >> lvl) == 1, m[x], 0.0), t_inv[x], inv_passes) for x in cs]
        t_inv = [t_inv[x] - _dot_hp(t_inv[x], z[x], inv_passes) for x in cs]

    egc = [jnp.exp(gcol[x]) for x in cs]
    uw = [_dot(t_inv[x], jnp.concatenate([v_of(x) * bcol[x], kb[x] * egc[x]], axis=1)) for x in cs]
    s_prev = [state_s[d, i] for d, i in chains]
    ws = [_dot(jnp.concatenate([uw[x][:, HEAD_DIM:], q_of(x) * egc[x]], axis=0), s_prev[x]) for x in cs]
    v_new = [uw[x][:, :HEAD_DIM] - ws[x][:c] for x in cs]
    av = [_dot(attn[x], v_new[x]) for x in cs]
    for x, (d, i) in enumerate(chains):
        o_refs[d][0, i] = ws[x][c:] + av[x]

    g_last = [jnp.broadcast_to(g_rows[x][:, last_idx[dir_of(x)]:last_idx[dir_of(x)] + 1], (c, c)) for x in cs]
    kdec_t = [(k_of(x) * jnp.exp(g_last[x] - gcol[x])).T for x in cs]
    upd = [_dot(kdec_t[x], v_new[x]) for x in cs]
    for x, (d, i) in enumerate(chains):
        state_s[d, i] = s_prev[x] * jnp.exp(g_last[x]) + upd[x]

    @pl.when(n == pl.num_programs(2) - 1)
    def _():
        sfin_ref[:, 0] = state_s[...]


def _gdn(q, k, v, ba_t, alog_b, dtb_b, s0):
    b, h, l, dh = q.shape
    c = GDN_CHUNK
    nch = l // c
    hg = GDN_HEADS_PER_STEP
    fwd = pl.BlockSpec((1, hg, c, dh), lambda i, g, nn: (i, g, nn, 0))
    bwd = pl.BlockSpec((1, hg, c, dh), lambda i, g, nn: (i, g, nch - 1 - nn, 0))
    st_spec = pl.BlockSpec((2, 1, hg, dh, dh), lambda i, g, nn: (0, i, g, 0, 0))
    par_spec = pl.BlockSpec((2, h, c), lambda i, g, nn: (0, 0, 0))
    hm = jax.ShapeDtypeStruct((b, h, l, dh), F32)
    return pl.pallas_call(
        functools.partial(_gdn_kernel, heads_per_step=hg, inv_passes=INV_PASSES),
        grid=(b, h // hg, nch),
        in_specs=[fwd, fwd, fwd, bwd, bwd, bwd,
                  pl.BlockSpec((1, 1, 2, 1, h, c), lambda i, g, nn: (i, nn, 0, 0, 0, 0)),
                  pl.BlockSpec((1, 1, 2, 1, h, c), lambda i, g, nn: (i, nch - 1 - nn, 0, 1, 0, 0)),
                  par_spec, par_spec, st_spec],
        out_specs=[fwd, bwd, st_spec],
        out_shape=[hm, hm, jax.ShapeDtypeStruct((2, b, h, dh, dh), F32)],
        scratch_shapes=[pltpu.VMEM((2, hg, dh, dh), F32),
                        pltpu.VMEM((2, h, c), F32),
                        pltpu.VMEM((2, h, c), F32)],
        name="gdn",
        compiler_params=_params(("arbitrary", "arbitrary", "arbitrary")),
    )(q, k, v, q, k, v, ba_t, ba_t, alog_b, dtb_b, s0)


MIX_TOKENS = 256


def _mix_kernel(of_ref, ob_ref, z_ref, ga_ref, gb_ref, gc_ref, u_ref, vg_ref, p_ref, x_ref,
                g1_ref, sh2_ref, sc2_ref, gnw_ref, lnw_ref, lnb_ref, ws_ref, bs_ref,
                pm_ref, icnt_ref, pw_ref, psc_ref, wa_ref, wb_ref, wc_ref, wo_ref,
                n2w_ref, wr_ref, x1_ref, h2_ref, aff_ref):
    tm = x_ref.shape[1]

    z = z_ref[0]
    gnw = gnw_ref[...]
    ya = []
    for h in range(HEADS):
        oh = of_ref[0, h] + ob_ref[0, h]
        zh = z[:, h * HEAD_DIM:(h + 1) * HEAD_DIM]
        ya.append(_rms(oh, gnw) * _silu(zh))
    pa = _dot(jnp.concatenate(ya, axis=1), wa_ref[...])

    u = _gelu_tanh(u_ref[0])
    vg = _gelu_tanh(vg_ref[0])
    mu = jnp.mean(vg, axis=-1, keepdims=True)
    var = jnp.mean(jnp.square(vg - mu), axis=-1, keepdims=True)
    vg = (vg - mu) * lax.rsqrt(var + NORM_EPS) * lnw_ref[...] + lnb_ref[...]
    gd = CMLP_WIDTH // CMLP_GROUPS
    rows = []
    for ch in range(tm // CMLP_CHUNK):
        r = slice(ch * CMLP_CHUNK, (ch + 1) * CMLP_CHUNK)
        cols = [_dot(ws_ref[g], vg[r, g * gd:(g + 1) * gd]) + bs_ref[g] for g in range(CMLP_GROUPS)]
        rows.append(jnp.concatenate(cols, axis=1))
    pb = _dot(u * jnp.concatenate(rows, axis=0), wb_ref[...])

    p = p_ref[0]
    psc = psc_ref[...]
    pg = POOL_WIDTH // len(POOL_WINDOWS)
    yc = []
    for g in range(len(POOL_WINDOWS)):
        pcol = p[:, g * pg:(g + 1) * pg]
        pooled = _dot_exact_rhs_lhs(pm_ref[g], pcol) * icnt_ref[g] - pcol
        yc.append(_dot(pooled, pw_ref[g]) * psc[:, g * pg:(g + 1) * pg])
    pc = _dot(jnp.concatenate(yc, axis=1), wc_ref[...])

    merged = _sigmoid(ga_ref[0]) * pa + _sigmoid(gb_ref[0]) * pb + _sigmoid(gc_ref[0]) * pc
    x1 = x_ref[0] + g1_ref[0] * _dot(merged, wo_ref[...])
    x1_ref[0] = x1

    h2 = _rms(x1, n2w_ref[...]) * (1.0 + sc2_ref[0]) + sh2_ref[0]
    h2_ref[0] = h2.astype(BF16)
    logits = _dot_hp(h2, wr_ref[...], 3)
    lane = lax.broadcasted_iota(I32, logits.shape, 1)
    logits = jnp.where(lane < N_EXPERTS, logits, -jnp.inf)
    e = jnp.exp(logits - jnp.max(logits, axis=-1, keepdims=True))
    aff_ref[0] = e / jnp.sum(e, axis=-1, keepdims=True)


def _dot_exact_rhs_lhs(a_bf16, b):
    b1, b2, b3 = _split3(b)
    mm = lambda y: jnp.dot(a_bf16, y, preferred_element_type=F32)
    return mm(b1) + (mm(b2) + mm(b3))


def _pool_constants(tm, seg):
    t = np.arange(tm)
    pos = t % seg
    base = t - pos
    mats, inv = [], []
    for w in POOL_WINDOWS:
        lo = np.clip(pos - w // 2, 0, seg) + base
        hi = np.clip(pos + w // 2, 0, seg) + base
        s = np.arange(tm)[None, :]
        mats.append(((s >= lo[:, None]) & (s < hi[:, None])).astype(np.float32))
        inv.append(np.broadcast_to((1.0 / (hi - lo).astype(np.float32))[:, None], (tm, POOL_WIDTH // len(POOL_WINDOWS))))
    return jnp.asarray(np.stack(mats), BF16), jnp.asarray(np.stack(inv), F32)


def _mix(o, proj, x, mod, lw, seg):
    b, l, d = x.shape
    tm = min(l, MIX_TOKENS)
    pm, icnt = _pool_constants(tm, seg)
    tok = lambda width, blk: pl.BlockSpec((1, tm, width), lambda i, t: (i, t, blk))
    modv = lambda blk: pl.BlockSpec((1, 1, d), lambda i, t: (i, 0, blk))
    full = lambda a: pl.BlockSpec(a.shape, lambda i, t: (0,) * a.ndim)
    consts = [lw['gdn_norm_w'], lw['cmlp_ln_w'], lw['cmlp_ln_b'], lw['cmlp_w_s'], lw['cmlp_b_s'],
              pm, icnt, lw['pool_w'], lw['pool_scale'], lw['w_br_a'], lw['w_br_b'], lw['w_br_c'],
              lw['w_out'], lw['norm2_w'], lw['w_router']]
    return pl.pallas_call(
        _mix_kernel,
        grid=(b, l // tm),
        in_specs=[pl.BlockSpec((1, HEADS, tm, HEAD_DIM), lambda i, t: (i, 0, t, 0)),
                  pl.BlockSpec((1, HEADS, tm, HEAD_DIM), lambda i, t: (i, 0, t, 0)),
                  tok(GDN_WIDTH, COL_Z // GDN_WIDTH),
                  tok(d, COL_GATES // d), tok(d, COL_GATES // d + 1), tok(d, COL_GATES // d + 2),
                  tok(CMLP_WIDTH, COL_U // CMLP_WIDTH), tok(CMLP_WIDTH, COL_VG // CMLP_WIDTH),
                  tok(POOL_WIDTH, COL_P // POOL_WIDTH),
                  tok(d, 0),
                  modv(2), modv(3), modv(4)] + [full(a) for a in consts],
        out_specs=[pl.BlockSpec((1, tm, d), lambda i, t: (i, t, 0)),
                   pl.BlockSpec((1, tm, d), lambda i, t: (i, t, 0)),
                   pl.BlockSpec((1, tm, LANES), lambda i, t: (i, t, 0))],
        out_shape=[jax.ShapeDtypeStruct((b, l, d), F32),
                   jax.ShapeDtypeStruct((b, l, d), BF16),
                   jax.ShapeDtypeStruct((b, l, LANES), F32)],
        name="mix",
        compiler_params=_params(("arbitrary", "arbitrary")),
    )(o[0], o[1], proj, proj, proj, proj, proj, proj, proj, x, mod, mod, mod, *consts)


F32_INF_BITS = 0x7F800000


ROUTE_ROWS = 256


def _route_kernel(a_ref, u_ref, slot_ref, *, cap):
    bits = pltpu.bitcast(a_ref[...], I32)
    capf = float(cap)

    def body(_, carry):
        lo, hi = carry
        mid = lo + ((hi - lo + 1) >> 1)
        cnt = jnp.sum(jnp.where(bits >= mid, 1.0, 0.0), axis=1, keepdims=True)
        ge = cnt >= capf
        return jnp.where(ge, mid, lo), jnp.where(ge, hi, mid - 1)

    rows = bits.shape[0]
    lo, _ = lax.fori_loop(0, 31, body, (jnp.zeros((rows, 1), I32),
                                        jnp.full((rows, 1), F32_INF_BITS, I32)))
    gt = bits > lo
    eq = bits == lo
    need = capf - jnp.sum(jnp.where(gt, 1.0, 0.0), axis=1, keepdims=True)
    upper = u_ref[...]
    eq_before = jnp.dot(jnp.where(eq, 1.0, 0.0).astype(BF16), upper, preferred_element_type=F32)
    sel = gt | (eq & (eq_before < need))
    pos = jnp.dot(jnp.where(sel, 1.0, 0.0).astype(BF16), upper, preferred_element_type=F32)
    slot_ref[...] = jnp.where(sel, pos.astype(I32), -1)


def _route(aff_t, cap):
    b, e, n = aff_t.shape
    rows = min(b * e, ROUTE_ROWS)
    upper = jnp.asarray(np.triu(np.ones((n, n), np.float32), 1), BF16)
    slots = pl.pallas_call(
        functools.partial(_route_kernel, cap=cap),
        grid=(b * e // rows,),
        in_specs=[pl.BlockSpec((rows, n), lambda i: (i, 0)),
                  pl.BlockSpec((n, n), lambda i: (0, 0))],
        out_specs=pl.BlockSpec((rows, n), lambda i: (i, 0)),
        out_shape=jax.ShapeDtypeStruct((b * e, n), I32),
        name="route",
        compiler_params=_params(("arbitrary",)),
    )(aff_t.reshape(b * e, n), upper)
    return slots.reshape(b, e, n)


MOE_SCATTER_ROWS = 512
MOE_SET_TOKENS = 2048


MOE_EXPERTS_PER_STEP = 1


def _moe_kernel(h_ref, slot_t_ref, slot_c_ref, aff_ref, wg_ref, wu_ref, wd_ref, o_ref, *, cap):
    step = pl.program_id(1)
    n = h_ref.shape[1]
    es = range(MOE_EXPERTS_PER_STEP)
    experts = [step * MOE_EXPERTS_PER_STEP + i for i in es]

    @pl.when(step == 0)
    def _():
        o_ref[...] = jnp.zeros_like(o_ref)

    slot_rows = lax.broadcasted_iota(I32, (cap, n), 0)
    sel = [jnp.where(slot_t_ref[0, pl.ds(e, 1), :] == slot_rows, 1.0, 0.0).astype(BF16) for e in experts]
    xe = [jnp.dot(sel[i], h_ref[0], preferred_element_type=F32).astype(BF16) for i in es]
    gate = [jnp.dot(xe[i], wg_ref[0, i], preferred_element_type=F32) for i in es]
    up = [jnp.dot(xe[i], wu_ref[0, i], preferred_element_type=F32) for i in es]
    ye = [_dot(_silu(gate[i]) * up[i], wd_ref[0, i]).astype(BF16) for i in es]

    tr = min(n, MOE_SCATTER_ROWS)
    lane = lax.broadcasted_iota(I32, (tr, N_EXPERTS), 1)
    slot_iota = lax.broadcasted_iota(I32, (tr, cap), 1).astype(F32)
    for r in range(n // tr):
        rs = pl.ds(r * tr, tr)
        total = None
        for i, e in enumerate(experts):
            scol = jnp.sum(jnp.where(lane == e, slot_c_ref[0, rs, :].astype(F32), 0.0), axis=1, keepdims=True)
            acol = jnp.sum(jnp.where(lane == e, aff_ref[0, rs, :], 0.0), axis=1, keepdims=True)
            put = jnp.where(scol == slot_iota, 1.0, 0.0).astype(BF16)
            part = acol * jnp.dot(put, ye[i], preferred_element_type=F32)
            total = part if total is None else total + part
        o_ref[0, rs, :] += total


def _moe(h2, slot_t, slot_c, aff, wg, wu, wd, layer, cap):
    g, n, d = h2.shape
    ff = wg.shape[3]
    per = MOE_EXPERTS_PER_STEP
    return pl.pallas_call(
        functools.partial(_moe_kernel, cap=cap),
        grid=(g, N_EXPERTS // per),
        in_specs=[pl.BlockSpec((1, n, d), lambda i, e: (i, 0, 0)),
                  pl.BlockSpec((1, N_EXPERTS, n), lambda i, e: (i, 0, 0)),
                  pl.BlockSpec((1, n, N_EXPERTS), lambda i, e: (i, 0, 0)),
                  pl.BlockSpec((1, n, N_EXPERTS), lambda i, e: (i, 0, 0)),
                  pl.BlockSpec((1, per, d, ff), lambda i, e: (layer, e, 0, 0)),
                  pl.BlockSpec((1, per, d, ff), lambda i, e: (layer, e, 0, 0)),
                  pl.BlockSpec((1, per, ff, d), lambda i, e: (layer, e, 0, 0))],
        out_specs=pl.BlockSpec((1, n, d), lambda i, e: (i, 0, 0)),
        out_shape=jax.ShapeDtypeStruct((g, n, d), F32),
        name="moe",
        compiler_params=_params(("arbitrary", "arbitrary")),
    )(h2, slot_t, slot_c, aff, wg, wu, wd)


def _resid_kernel(x_ref, y_ref, g_ref, w_ref, o_ref, *, final_norm):
    x = x_ref[0] + g_ref[0] * y_ref[0]
    o_ref[0] = _rms(x, w_ref[...]) if final_norm else x


def _residual(x1, moe, mod, final_w, final_norm, merge):
    b, l, d = x1.shape
    tm = min(l, 512)
    tok = pl.BlockSpec((1, tm, d), lambda i, t: (i, t, 0))
    return pl.pallas_call(
        functools.partial(_resid_kernel, final_norm=final_norm),
        grid=(b, l // tm),
        in_specs=[tok,
                  pl.BlockSpec((1, tm, d), lambda i, t: (i // merge, (i % merge) * (l // tm) + t, 0)),
                  pl.BlockSpec((1, 1, d), lambda i, t: (i, 0, 5)),
                  pl.BlockSpec((1, d), lambda i, t: (0, 0))],
        out_specs=tok,
        out_shape=jax.ShapeDtypeStruct((b, l, d), F32),
        name="residual",
        compiler_params=_params(("arbitrary", "arbitrary")),
    )(x1, moe, mod, final_w.reshape(1, d))


def _ba_rows(proj, col):
    b, l, _ = proj.shape
    ba = proj[:, :, col:col + 4 * HEADS].reshape(b, l // GDN_CHUNK, GDN_CHUNK, 2, 2, HEADS)
    return jnp.transpose(ba, (0, 1, 3, 4, 5, 2))


def _channel_ffn(h2, aff, x1, mod, lw, final_w, final_norm):
    b, n, d = h2.shape
    cap = EC_CAPACITY * n // N_EXPERTS
    aff_t = jnp.swapaxes(aff[:, :, :N_EXPERTS], 1, 2)
    slot_t = _route(aff_t, cap)
    merge = max(1, min(b, MOE_SET_TOKENS // n))
    while b % merge:
        merge -= 1
    if merge > 1:
        offset = (jnp.arange(b, dtype=I32) % merge * cap)[:, None, None]
        slot_t = jnp.where(slot_t >= 0, slot_t + offset, -1)
        join = lambda a: jnp.swapaxes(a.reshape(b // merge, merge, N_EXPERTS, n), 1, 2).reshape(
            b // merge, N_EXPERTS, merge * n)
        slot_t = join(slot_t)
    sets = lambda a: a.reshape(b // merge, merge * n, a.shape[-1])
    moe = _moe(sets(h2), slot_t, jnp.swapaxes(slot_t, 1, 2), sets(aff[:, :, :N_EXPERTS]),
               lw['w_gate'], lw['w_up'], lw['w_down'], lw['layer'], cap * merge)
    return _residual(x1, moe, mod, final_w, final_norm, merge)


def kernel(x, c, ctx, c_ctx, w_ada, b_ada, norm1_w, norm2_w, w_in, qkv_conv_w, gdn_a_log, gdn_dt_bias, gdn_norm_w, cmlp_ln_w, cmlp_ln_b, cmlp_w_s, cmlp_b_s, pool_w, pool_scale, w_br_a, w_br_b, w_br_c, w_out, w_router, w_gate, w_up, w_down, final_norm_w):
    b, l, d = x.shape
    depth = w_ada.shape[0]
    assert l % GDN_CHUNK == 0 and ctx.shape[1] % GDN_CHUNK == 0 and l % GRID_W == 0

    cc = jnp.concatenate([c, c_ctx[None, :], jnp.zeros((31 - b, d), F32)], axis=0)
    mods = _modulation(cc, w_ada, b_ada)
    zero_states = jnp.zeros((2, b, HEADS, HEAD_DIM, HEAD_DIM), F32)
    w_gate_bf, w_up_bf, w_down_bf = w_gate.astype(BF16), w_up.astype(BF16), w_down.astype(BF16)

    for layer in range(depth):
        last = layer == depth - 1
        mod_x = mods[layer, :b][:, None, :]
        mod_c = jnp.broadcast_to(mods[layer, b][None, None, :], mod_x.shape)

        wi = w_in[layer]
        wp = jnp.concatenate([wi[:, 0:4096], wi[:, 5664:8736], wi[:, 4128:5664], wi[:, 4096:4128],
                              jnp.zeros((d, PROJ_COLS - COL_BA - 4 * HEADS), F32)], axis=1).astype(BF16)
        wr = jnp.concatenate([w_router[layer], jnp.zeros((d, LANES - N_EXPERTS), F32)], axis=1)
        lw = {
            'gdn_norm_w': gdn_norm_w[layer].reshape(1, HEAD_DIM),
            'cmlp_ln_w': cmlp_ln_w[layer].reshape(1, CMLP_WIDTH),
            'cmlp_ln_b': cmlp_ln_b[layer].reshape(1, CMLP_WIDTH),
            'cmlp_w_s': cmlp_w_s[layer].astype(BF16),
            'cmlp_b_s': jnp.broadcast_to(cmlp_b_s[layer][:, :, None],
                                         (CMLP_GROUPS, CMLP_CHUNK, CMLP_WIDTH // CMLP_GROUPS)),
            'pool_w': pool_w[layer].astype(BF16),
            'pool_scale': pool_scale[layer].reshape(1, POOL_WIDTH),
            'w_br_a': w_br_a[layer].astype(BF16), 'w_br_b': w_br_b[layer].astype(BF16),
            'w_br_c': w_br_c[layer].astype(BF16), 'w_out': w_out[layer].astype(BF16),
            'norm2_w': norm2_w[layer].reshape(1, d), 'w_router': wr,
            'w_gate': w_gate_bf, 'w_up': w_up_bf, 'w_down': w_down_bf, 'layer': layer,
        }
        alog_b = jnp.broadcast_to(gdn_a_log[layer][:, :, None], (2, HEADS, GDN_CHUNK))
        dtb_b = jnp.broadcast_to(gdn_dt_bias[layer][:, :, None], (2, HEADS, GDN_CHUNK))

        def gdn_path(tokens, mod, states, shared_mod, states_only):
            nb, nl, _ = tokens.shape
            if shared_mod:
                per = max(1, min(nb, INPROJ_SET_TOKENS // nl))
                while nb % per:
                    per -= 1
                proj = _in_projection(tokens.reshape(nb // per, per * nl, d), mod[:nb // per], norm1_w[layer], wp,
                                      states_only)
                proj = proj.reshape(nb, nl, proj.shape[-1])
            else:
                proj = _in_projection(tokens, mod, norm1_w[layer], wp, states_only)
            ba_col = 2 * PROJ_TN + COL_BA % PROJ_TN if states_only else COL_BA
            q, k, v = _qkv_conv(proj, qkv_conv_w[layer])
            o_f, o_b, s_fin = _gdn(q, k, v, _ba_rows(proj, ba_col), alog_b, dtb_b, states)
            return proj, (o_f, o_b), s_fin

        proj_c, o_c, ctx_states = gdn_path(ctx, mod_c, zero_states, True, last)
        proj_x, o_x, _ = gdn_path(x, mod_x, ctx_states, False, False)
        x1, h2, aff = _mix(o_x, proj_x, x, mod_x, lw, GRID_W)
        x = _channel_ffn(h2, aff, x1, mod_x, lw, final_norm_w, last)
        if not last:
            c1, h2c, affc = _mix(o_c, proj_c, ctx, mod_c, lw, ctx.shape[1])
            ctx = _channel_ffn(h2c, affc, c1, mod_c, lw, final_norm_w, False)
    return x
```
